```python
import math, functools
import jax, jax.numpy as jnp
from jax import lax
import numpy as np

D_MODEL = 1024
BATCH = 8
SEQ = 4096
DEPTH = 1
DEC_BATCH = 128
DEC_SEQ = 4
PAST_LEN = 8192
PAGE_SIZE = 128

HEAD_DIM = 64
D_MIX = D_MODEL
H_SB = 8
H_NSA = 8
G_NSA = 2
HPG = H_NSA // G_NSA
CMP_BLOCK = 32
CMP_STRIDE = 16
SEL_BLOCK = 64
N_SELECT = 16
WINDOW = 512
Q_BLOCK = 128
D_FF = 2816
N_MOD = 9
LN_EPS = 1e-5
FORCE_BONUS = 1.0e4
NEG = -1.0e30
DN_ALPHA = (2.0 * DEPTH) ** 0.25
DN_BETA = (8.0 * DEPTH) ** -0.25

SB_W = H_SB * HEAD_DIM
NSA_QW = H_NSA * HEAD_DIM
KV_W = G_NSA * HEAD_DIM
OFF_NSA_Q = 3 * SB_W
OFF_NSA_KV = OFF_NSA_Q + NSA_QW
OFF_WIN = OFF_NSA_KV + 4 * KV_W
OFF_GATE = OFF_WIN + 2 * KV_W
N_IN = OFF_GATE + 3 * H_NSA

kernel_name = 'hymba_stickbreak_nsa_macaron_deepnorm_step'


def alibi_slopes(n):
    return jnp.exp2(-8.0 * jnp.arange(1, n + 1, dtype=jnp.float32) / n)


def layer_norm(x, g, b):
    xf = x.astype(jnp.float32)
    mu = jnp.mean(xf, -1, keepdims=True)
    var = jnp.mean(jnp.square(xf - mu), -1, keepdims=True)
    y = (xf - mu) * lax.rsqrt(var + LN_EPS) * g.astype(jnp.float32) + b.astype(jnp.float32)
    return y.astype(x.dtype)


def swiglu(u, w_up, w_down):
    g, v = jnp.split(u @ w_up, 2, axis=-1)
    return (jax.nn.silu(g) * v) @ w_down


def masked_softmax(s, mask, axis):
    s = jnp.where(mask, s, NEG)
    m = jnp.max(s, axis=axis, keepdims=True)
    e = jnp.where(mask, jnp.exp(s - m), 0.0)
    return e / jnp.maximum(jnp.sum(e, axis=axis, keepdims=True), 1e-30)


def split_proj(h):
    lead = h.shape[:-1]
    q_sb = h[..., :SB_W].reshape(*lead, H_SB, HEAD_DIM)
    kv_sb = h[..., SB_W:3 * SB_W].reshape(*lead, 2, H_SB, HEAD_DIM)
    q_nsa = h[..., OFF_NSA_Q:OFF_NSA_KV].reshape(*lead, H_NSA, HEAD_DIM)
    kv_nsa = h[..., OFF_NSA_KV:OFF_WIN].reshape(*lead, 4, G_NSA, HEAD_DIM)
    kv_win = h[..., OFF_WIN:OFF_GATE].reshape(*lead, 2, G_NSA, HEAD_DIM)
    gate = jax.nn.sigmoid(h[..., OFF_GATE:].astype(jnp.float32)).reshape(*lead, H_NSA, 3)
    return q_sb, kv_sb, q_nsa, kv_nsa, kv_win, gate


def compress(kv, w):
    n_chunk = kv.shape[0] // CMP_STRIDE
    chunks = kv[:n_chunk * CMP_STRIDE].reshape(n_chunk, CMP_STRIDE, G_NSA, HEAD_DIM).astype(jnp.float32)
    w2 = w.astype(jnp.float32).reshape(CMP_BLOCK // CMP_STRIDE, CMP_STRIDE, HEAD_DIM, HEAD_DIM)
    head = jnp.einsum('nigd,ide->nge', chunks, w2[0])
    tail = jnp.einsum('nigd,ide->nge', chunks, w2[1])
    ends = jnp.arange(n_chunk - 1, dtype=jnp.int32) * CMP_STRIDE + (CMP_BLOCK - 1)
    return head[:-1] + tail[1:], ends


def mix_tokens(qpos, q_sb, kv_sb, q_nsa, gate, cmp_k, cmp_v, cmp_end, slc_k, slc_v, win_kv, win_pos, slopes):
    f32 = jnp.float32
    scale = HEAD_DIM ** -0.5
    nq = qpos.shape[0]

    kpos = jnp.arange(kv_sb.shape[0], dtype=jnp.int32)
    z = jnp.einsum('qhd,khd->hqk', q_sb.astype(f32), kv_sb[:, 0].astype(f32)) * scale
    causal = kpos[None, None, :] < qpos[None, :, None]
    log_1mb = jnp.where(causal, jax.nn.log_sigmoid(-z), 0.0)
    between = lax.cumsum(log_1mb, axis=2, reverse=True) - log_1mb
    a = jnp.where(causal, jnp.exp(jax.nn.log_sigmoid(z) + between), 0.0)
    o_sb = jnp.einsum('hqk,khd->qhd', a, kv_sb[:, 1].astype(f32))

    qn = q_nsa.astype(f32).reshape(nq, G_NSA, HPG, HEAD_DIM)
    m4 = slopes.reshape(G_NSA, HPG)[None, :, :, None]
    dist_c = (qpos[:, None] - cmp_end[None, :]).astype(f32)[:, None, None, :]
    mask_c = (cmp_end[None, :] <= qpos[:, None])[:, None, None, :]
    s_c = jnp.einsum('qghd,cgd->qghc', qn, cmp_k) * scale - m4 * dist_c
    p_c = masked_softmax(s_c, mask_c, -1)
    o_c = jnp.einsum('qghc,cgd->qghd', p_c, cmp_v)

    n_sb = slc_k.shape[0] // SEL_BLOCK
    k_sel = min(N_SELECT, n_sb)
    n_c = cmp_end.shape[0]
    n_chunk = n_sb * (SEL_BLOCK // CMP_STRIDE)
    imp = p_c.sum(axis=2)
    imp_chunk = (jnp.pad(imp, ((0, 0), (0, 0), (0, n_chunk - n_c)))
                 + jnp.pad(imp, ((0, 0), (0, 0), (1, n_chunk - n_c - 1))))
    imp_blk = imp_chunk.reshape(nq, G_NSA, n_sb, SEL_BLOCK // CMP_STRIDE).sum(-1)
    blk = jnp.arange(n_sb, dtype=jnp.int32)
    tb = (qpos // SEL_BLOCK)[:, None, None]
    valid = (blk * SEL_BLOCK)[None, None, :] <= qpos[:, None, None]
    forced = (blk == 0) | (blk == tb) | (blk == tb - 1)
    score = jnp.where(valid, imp_blk + jnp.where(forced, FORCE_BONUS, 0.0), -jnp.inf)
    top_val, top_idx = lax.top_k(score, k_sel)
    sel_ok = jnp.isfinite(top_val)
    kb = jnp.moveaxis(slc_k.reshape(n_sb, SEL_BLOCK, G_NSA, HEAD_DIM), 2, 0)
    vb = jnp.moveaxis(slc_v.reshape(n_sb, SEL_BLOCK, G_NSA, HEAD_DIM), 2, 0)
    g_idx = jnp.arange(G_NSA)[None, :, None]
    k_g = kb[g_idx, top_idx].astype(f32)
    v_g = vb[g_idx, top_idx].astype(f32)
    pos = top_idx[..., None] * SEL_BLOCK + jnp.arange(SEL_BLOCK, dtype=jnp.int32)
    dist_s = qpos[:, None, None, None] - pos
    mask_s = sel_ok[:, :, None, :, None] & (dist_s >= 0)[:, :, None]
    s_s = (jnp.einsum('qghd,qgkbd->qghkb', qn, k_g) * scale
           - m4[..., None] * dist_s.astype(f32)[:, :, None])
    p_s = masked_softmax(s_s, mask_s, (3, 4))
    o_s = jnp.einsum('qghkb,qgkbd->qghd', p_s, v_g)

    dist_w = qpos[:, None] - win_pos[None, :]
    mask_w = ((dist_w >= 0) & (dist_w < WINDOW) & (win_pos >= 0)[None, :])[:, None, None, :]
    s_w = (jnp.einsum('qghd,wgd->qghw', qn, win_kv[:, 0].astype(f32)) * scale
           - m4 * dist_w.astype(f32)[:, None, None, :])
    p_w = masked_softmax(s_w, mask_w, -1)
    o_w = jnp.einsum('qghw,wgd->qghd', p_w, win_kv[:, 1].astype(f32))

    g = gate.reshape(nq, G_NSA, HPG, 3)
    o_n = g[..., 0:1] * o_c + g[..., 1:2] * o_s + g[..., 2:3] * o_w
    return jnp.concatenate([o_sb.reshape(nq, SB_W), o_n.reshape(nq, NSA_QW)], axis=-1)


def prompt_mix(h, w_cmp_k, w_cmp_v, slopes):
    q_sb, kv_sb, q_nsa, kv_nsa, kv_win, gate = split_proj(h)
    s_len = h.shape[1]
    n_qb = s_len // Q_BLOCK

    def one_seq(args):
        q_sb1, kv_sb1, q_nsa1, kv_nsa1, kv_win1, gate1 = args
        cmp_k, cmp_end = compress(kv_nsa1[:, 0], w_cmp_k)
        cmp_v, _ = compress(kv_nsa1[:, 1], w_cmp_v)
        win_pad = jnp.pad(kv_win1, ((WINDOW, 0), (0, 0), (0, 0), (0, 0)))

        def one_block(qb):
            q0 = qb * Q_BLOCK
            qpos = q0 + jnp.arange(Q_BLOCK, dtype=jnp.int32)
            sl = lambda arr: lax.dynamic_slice_in_dim(arr, q0, Q_BLOCK, 0)
            win = lax.dynamic_slice_in_dim(win_pad, q0, WINDOW + Q_BLOCK, 0)
            win_pos = q0 - WINDOW + jnp.arange(WINDOW + Q_BLOCK, dtype=jnp.int32)
            return mix_tokens(qpos, sl(q_sb1), kv_sb1, sl(q_nsa1), sl(gate1), cmp_k, cmp_v, cmp_end,
                              kv_nsa1[:, 2], kv_nsa1[:, 3], win, win_pos, slopes)

        return lax.map(one_block, jnp.arange(n_qb, dtype=jnp.int32)).reshape(s_len, D_MIX)

    o = lax.map(one_seq, (q_sb, kv_sb, q_nsa, kv_nsa, kv_win, gate))
    n_w = min(WINDOW, s_len)
    return o.astype(h.dtype), (kv_sb, kv_nsa, kv_win[:, s_len - n_w:])


def sample_mix(h, layer, cache_sb_kv, cache_nsa_kv, state_win, page_table, w_cmp_k, w_cmp_v, slopes):
    q_sb, kv_sb, q_nsa, kv_nsa, kv_win, gate = split_proj(h)
    n_new = h.shape[1]
    past = page_table.shape[1] * cache_sb_kv.shape[2]
    n_w = state_win.shape[1]
    t_all = past + n_new
    t_pad = -(-t_all // SEL_BLOCK) * SEL_BLOCK
    qpos = past + jnp.arange(n_new, dtype=jnp.int32)
    win_pos = past - n_w + jnp.arange(n_w + n_new, dtype=jnp.int32)

    def one_seq(args):
        q_sb1, kv_sb1, q_nsa1, kv_nsa1, kv_win1, gate1, pages, win_state = args
        sb_all = jnp.concatenate(
            [cache_sb_kv[layer, pages].reshape(past, 2, H_SB, HEAD_DIM), kv_sb1], axis=0)
        nsa_all = jnp.concatenate(
            [cache_nsa_kv[layer, pages].reshape(past, 4, G_NSA, HEAD_DIM), kv_nsa1], axis=0)
        cmp_k, cmp_end = compress(nsa_all[:, 0], w_cmp_k)
        cmp_v, _ = compress(nsa_all[:, 1], w_cmp_v)
        slc = jnp.pad(nsa_all[:, 2:4], ((0, t_pad - t_all), (0, 0), (0, 0), (0, 0)))
        win_all = jnp.concatenate([win_state, kv_win1], axis=0)
        o = mix_tokens(qpos, q_sb1, sb_all, q_nsa1, gate1, cmp_k, cmp_v, cmp_end,
                       slc[:, 0], slc[:, 1], win_all, win_pos, slopes)
        return o, win_all[n_new:]

    o, new_win = lax.map(one_seq, (q_sb, kv_sb, q_nsa, kv_nsa, kv_win, gate, page_table, state_win))
    return o.astype(h.dtype), (kv_sb, kv_nsa, new_win)


def decoder_layer(x, c, mix_fn, w_ada, b_ada, w_ff1_up, w_ff1_down, w_in, w_o,
                  w_ff2_up, w_ff2_down, ln_g, ln_b):
    mod = (jax.nn.silu(c) @ w_ada + b_ada).reshape(c.shape[0], 1, N_MOD, D_MODEL)

    def modulate(v, i):
        return v * (1.0 + mod[:, :, i + 1]) + mod[:, :, i]

    u = modulate(x, 0)
    x = layer_norm(DN_ALPHA * x + 0.5 * mod[:, :, 2] * swiglu(u, w_ff1_up, w_ff1_down), ln_g[0], ln_b[0])
    u = modulate(x, 3)
    o, state = mix_fn(u @ w_in)
    x = layer_norm(DN_ALPHA * x + mod[:, :, 5] * (o @ w_o), ln_g[1], ln_b[1])
    u = modulate(x, 6)
    x = layer_norm(DN_ALPHA * x + 0.5 * mod[:, :, 8] * swiglu(u, w_ff2_up, w_ff2_down), ln_g[2], ln_b[2])
    return x, state


def setup_inputs(seed: int = 0) -> dict:
    key = jax.random.key(seed)
    ks = jax.random.split(key, 20)
    f32 = jnp.float32
    n_pages = PAST_LEN // PAGE_SIZE
    n_used = DEC_BATCH * n_pages
    n_pool = n_used + n_used // 4
    n_win = min(WINDOW, PAST_LEN)

    def nrm(k, shape, s):
        return jax.random.normal(k, shape, f32) * s

    page_table = jax.random.permutation(ks[5], n_pool)[:n_used].reshape(DEC_BATCH, n_pages).astype(jnp.int32)
    return {
        'x_prompt': nrm(ks[0], (BATCH, SEQ, D_MODEL), 1.0),
        'x_sample': nrm(ks[1], (DEC_BATCH, DEC_SEQ, D_MODEL), 1.0),
        'cache_sb_kv': nrm(ks[2], (DEPTH, n_pool, PAGE_SIZE, 2, H_SB, HEAD_DIM), 1.0),
        'cache_nsa_kv': nrm(ks[3], (DEPTH, n_pool, PAGE_SIZE, 4, G_NSA, HEAD_DIM), 1.0),
        'state_win_kv': nrm(ks[4], (DEPTH, DEC_BATCH, n_win, 2, G_NSA, HEAD_DIM), 1.0),
        'page_table': page_table,
        'c_prompt': nrm(ks[6], (BATCH, D_MODEL), 1.0),
        'c_sample': nrm(ks[7], (DEC_BATCH, D_MODEL), 1.0),
        'w_ada': nrm(ks[8], (DEPTH, D_MODEL, N_MOD * D_MODEL), D_MODEL ** -0.5),
        'b_ada': nrm(ks[9], (DEPTH, N_MOD * D_MODEL), 0.02),
        'w_ff1_up': nrm(ks[10], (DEPTH, D_MODEL, 2 * D_FF), D_MODEL ** -0.5),
        'w_ff1_down': nrm(ks[11], (DEPTH, D_FF, D_MODEL), DN_BETA * D_FF ** -0.5),
        'w_in': nrm(ks[12], (DEPTH, D_MODEL, N_IN), D_MODEL ** -0.5),
        'w_cmp_k': nrm(ks[13], (DEPTH, CMP_BLOCK * HEAD_DIM, HEAD_DIM), (CMP_BLOCK * HEAD_DIM) ** -0.5),
        'w_cmp_v': nrm(ks[14], (DEPTH, CMP_BLOCK * HEAD_DIM, HEAD_DIM), (CMP_BLOCK * HEAD_DIM) ** -0.5),
        'w_o': nrm(ks[15], (DEPTH, D_MIX, D_MODEL), DN_BETA * D_MIX ** -0.5),
        'w_ff2_up': nrm(ks[16], (DEPTH, D_MODEL, 2 * D_FF), D_MODEL ** -0.5),
        'w_ff2_down': nrm(ks[17], (DEPTH, D_FF, D_MODEL), DN_BETA * D_FF ** -0.5),
        'ln_g': 1.0 + nrm(ks[18], (DEPTH, 3, D_MODEL), 0.02),
        'ln_b': nrm(ks[19], (DEPTH, 3, D_MODEL), 0.02),
    }


def reference(x_prompt, x_sample, cache_sb_kv, cache_nsa_kv, state_win_kv, page_table,
              c_prompt, c_sample, w_ada, b_ada, w_ff1_up, w_ff1_down, w_in, w_cmp_k, w_cmp_v,
              w_o, w_ff2_up, w_ff2_down, ln_g, ln_b):
    slopes = alibi_slopes(H_NSA)
    y_prompt, y_sample = x_prompt, x_sample
    p_sb, p_nsa, p_win, s_sb, s_nsa, s_win = [], [], [], [], [], []
    for layer in range(DEPTH):
        shared = (w_ada[layer], b_ada[layer], w_ff1_up[layer], w_ff1_down[layer], w_in[layer],
                  w_o[layer], w_ff2_up[layer], w_ff2_down[layer], ln_g[layer], ln_b[layer])
        pmix = functools.partial(prompt_mix, w_cmp_k=w_cmp_k[layer], w_cmp_v=w_cmp_v[layer], slopes=slopes)
        y_prompt, (a_sb, a_nsa, a_win) = decoder_layer(y_prompt, c_prompt, pmix, *shared)
        smix = functools.partial(sample_mix, layer=layer, cache_sb_kv=cache_sb_kv,
                                 cache_nsa_kv=cache_nsa_kv, state_win=state_win_kv[layer],
                                 page_table=page_table, w_cmp_k=w_cmp_k[layer],
                                 w_cmp_v=w_cmp_v[layer], slopes=slopes)
        y_sample, (b_sb, b_nsa, b_win) = decoder_layer(y_sample, c_sample, smix, *shared)
        p_sb.append(a_sb); p_nsa.append(a_nsa); p_win.append(a_win)
        s_sb.append(b_sb); s_nsa.append(b_nsa); s_win.append(b_win)
    new_sb_prompt = jnp.stack(p_sb)
    new_nsa_prompt = jnp.stack(p_nsa)
    new_win_prompt = jnp.stack(p_win)
    new_sb_sample = jnp.stack(s_sb)
    new_nsa_sample = jnp.stack(s_nsa)
    new_win_sample = jnp.stack(s_win)
    return (y_prompt, y_sample, new_sb_prompt, new_nsa_prompt, new_win_prompt,
            new_sb_sample, new_nsa_sample, new_win_sample)
```

```python
import functools

import jax
import jax.numpy as jnp
from jax import lax
from jax.experimental import pallas as pl
from jax.experimental.pallas import tpu as pltpu

F32 = jnp.float32
BF16 = jnp.bfloat16

HEAD_DIM = 64
H_SB = 8
H_NSA = 8
G_NSA = 2
HPG = H_NSA // G_NSA
CMP_BLOCK = 32
CMP_STRIDE = 16
SEL_BLOCK = 64
N_SELECT = 16
WINDOW = 512
N_MOD = 9
LN_EPS = 1e-5
FORCE_BONUS = 1.0e4
NEG = -1.0e30

SB_W = H_SB * HEAD_DIM
NSA_QW = H_NSA * HEAD_DIM
KV_W = G_NSA * HEAD_DIM
OFF_NSA_Q = 3 * SB_W
OFF_NSA_KV = OFF_NSA_Q + NSA_QW
OFF_WIN = OFF_NSA_KV + 4 * KV_W
OFF_GATE = OFF_WIN + 2 * KV_W
N_IN = OFF_GATE + 3 * H_NSA

LANES = 128
KEY_TILE = 128
VMEM_LIMIT = 56 * 1024 * 1024

TOKEN_TILE = 512
Q_TILE = 128
FF_CHUNK = 256
Q_PAD = 8


def _cparams(n_axes):
    return pltpu.CompilerParams(dimension_semantics=("arbitrary",) * n_axes,
                                vmem_limit_bytes=VMEM_LIMIT)


def _dot(a, b):
    return jnp.dot(a, b, preferred_element_type=F32)


def _dot_nt(a, b):
    return lax.dot_general(a, b, (((1,), (1,)), ((), ())), preferred_element_type=F32)


def _split_bf16(x, parts):
    out = []
    r = x
    for _ in range(parts):
        h = r.astype(BF16)
        out.append(h)
        r = r - h.astype(F32)
    return out


def _sigmoid(x):
    return 1.0 / (1.0 + jnp.exp(-x))


def _layer_norm(y, g, b):
    mu = jnp.mean(y, axis=-1, keepdims=True)
    d = y - mu
    var = jnp.mean(d * d, axis=-1, keepdims=True)
    return d * lax.rsqrt(var + LN_EPS) * g + b


def _ada_kernel(c_ref, w_ref, b_ref, o_ref):
    c = c_ref[...]
    a = c * _sigmoid(c)
    a_hi, a_lo = _split_bf16(a, 2)
    w_hi, w_lo = _split_bf16(w_ref[...], 2)
    o_ref[...] = _dot(a_hi, w_hi) + (_dot(a_hi, w_lo) + _dot(a_lo, w_hi)) + b_ref[...]


def _ada(c, w, b):
    n, d = c.shape
    n_out = w.shape[1]
    bn = 1024
    return pl.pallas_call(
        _ada_kernel,
        grid=(n_out // bn,),
        in_specs=[pl.BlockSpec((n, d), lambda j: (0, 0)),
                  pl.BlockSpec((d, bn), lambda j: (0, j)),
                  pl.BlockSpec((1, bn), lambda j: (0, j))],
        out_specs=pl.BlockSpec((n, bn), lambda j: (0, j)),
        out_shape=jax.ShapeDtypeStruct((n, n_out), F32),
        compiler_params=_cparams(1),
        name="ada",
    )(c, w, b.reshape(1, n_out))


def _mod_spec(per_token, tm, d, tiles_per_seq):
    if per_token:
        return pl.BlockSpec((N_MOD, None, tm, d), lambda i: (0, 0, i, 0))
    return pl.BlockSpec((N_MOD, None, 1, d), lambda i: (0, i // tiles_per_seq, 0, 0))


def _ffn_kernel(*refs, mod_base, d_ff, alpha, fuse_oproj):
    if fuse_oproj:
        (x_ref, osb_ref, onsa_ref, mod_ref, wosb_ref, wonsa_ref, wup_ref, wdn_ref,
         lng_ref, lnb_ref, o_ref, acc_ref) = refs
        mixed = _dot(osb_ref[...], wosb_ref[...]) + _dot(onsa_ref[...], wonsa_ref[...])
        x = _layer_norm(alpha * x_ref[...] + mod_ref[mod_base - 1] * mixed,
                        lng_ref[0:1, :], lnb_ref[0:1, :])
        ln_row = 1
    else:
        x_ref, mod_ref, wup_ref, wdn_ref, lng_ref, lnb_ref, o_ref, acc_ref = refs
        x = x_ref[...]
        ln_row = 0
    u = (x * (1.0 + mod_ref[mod_base + 1]) + mod_ref[mod_base]).astype(BF16)
    for c in range(d_ff // FF_CHUNK):
        lo = c * FF_CHUNK
        g = _dot(u, wup_ref[:, lo:lo + FF_CHUNK])
        v = _dot(u, wup_ref[:, d_ff + lo:d_ff + lo + FF_CHUNK])
        act = (g * _sigmoid(g) * v).astype(BF16)
        part = _dot(act, wdn_ref[lo:lo + FF_CHUNK, :])
        if c == 0:
            acc_ref[...] = part
        else:
            acc_ref[...] += part
    y = alpha * x + 0.5 * mod_ref[mod_base + 2] * acc_ref[...]
    o_ref[...] = _layer_norm(y, lng_ref[ln_row:ln_row + 1, :], lnb_ref[ln_row:ln_row + 1, :])


def _ffn(x, mod4, w_up, w_down, ln_g2, ln_b2, *, mod_base, alpha, per_token, tiles_per_seq,
         oproj=None):
    t, d = x.shape
    d_ff = w_down.shape[0]
    tm = min(TOKEN_TILE, t)
    row = lambda i: (i, 0)
    const = lambda i: (0, 0)
    in_specs = [pl.BlockSpec((tm, d), row)]
    args = [x]
    if oproj is not None:
        o_sb, o_nsa, w_osb, w_onsa = oproj
        in_specs += [pl.BlockSpec((tm, SB_W), row), pl.BlockSpec((tm, NSA_QW), row)]
        args += [o_sb, o_nsa]
    in_specs.append(_mod_spec(per_token, tm, d, tiles_per_seq))
    args.append(mod4)
    if oproj is not None:
        in_specs += [pl.BlockSpec((SB_W, d), const), pl.BlockSpec((NSA_QW, d), const)]
        args += [w_osb, w_onsa]
    in_specs += [pl.BlockSpec((d, 2 * d_ff), const), pl.BlockSpec((d_ff, d), const),
                 pl.BlockSpec(ln_g2.shape, const), pl.BlockSpec(ln_b2.shape, const)]
    args += [w_up, w_down, ln_g2, ln_b2]
    return pl.pallas_call(
        functools.partial(_ffn_kernel, mod_base=mod_base, d_ff=d_ff, alpha=alpha,
                          fuse_oproj=oproj is not None),
        grid=(t // tm,),
        in_specs=in_specs,
        out_specs=pl.BlockSpec((tm, d), row),
        out_shape=jax.ShapeDtypeStruct((t, d), F32),
        scratch_shapes=[pltpu.VMEM((tm, d), F32)],
        compiler_params=_cparams(1),
        name="ffn_oproj" if oproj is not None else "ffn",
    )(*args)


_P_QSB = 0
_P_KVSB = _P_QSB + H_SB * LANES
_P_QN = _P_KVSB + 2 * SB_W
_P_KVN = _P_QN + H_NSA * LANES
_P_WIN = _P_KVN + 4 * KV_W
_P_GATE = _P_WIN + 2 * KV_W
_P_END = _P_GATE + LANES


def _proj_weights(w_in):
    d = w_in.shape[0]
    scale = HEAD_DIM ** -0.5
    q_sb = (w_in[:, :SB_W] * scale).reshape(d, H_SB, HEAD_DIM)
    q_sb_pad = jnp.zeros((d, H_SB, LANES), F32)
    for h in range(H_SB):
        o = (h % 2) * HEAD_DIM
        q_sb_pad = q_sb_pad.at[:, h, o:o + HEAD_DIM].set(q_sb[:, h])
    q_n = (w_in[:, OFF_NSA_Q:OFF_NSA_KV] * scale).reshape(d, H_NSA, HEAD_DIM)
    q_n_pad = jnp.zeros((d, H_NSA, LANES), F32)
    for h in range(H_NSA):
        o = (h // HPG) * HEAD_DIM
        q_n_pad = q_n_pad.at[:, h, o:o + HEAD_DIM].set(q_n[:, h])
    gate = jnp.pad(w_in[:, OFF_GATE:], ((0, 0), (0, LANES - 3 * H_NSA)))
    w = jnp.concatenate([q_sb_pad.reshape(d, -1), w_in[:, SB_W:3 * SB_W], q_n_pad.reshape(d, -1),
                         w_in[:, OFF_NSA_KV:OFF_WIN], w_in[:, OFF_WIN:OFF_GATE], gate], axis=1)
    return w.astype(BF16)


def _proj_kernel(x_ref, mod_ref, w_ref, qsb_ref, kvsb_ref, kvsbb_ref, qn_ref, kvn_ref, kvnb_ref,
                 win_ref, winb_ref, gate_ref, *, mod_base):
    u = (x_ref[...] * (1.0 + mod_ref[mod_base + 1]) + mod_ref[mod_base]).astype(BF16)
    qsb_ref[...] = _dot(u, w_ref[:, _P_QSB:_P_KVSB]).astype(BF16)
    kv = _dot(u, w_ref[:, _P_KVSB:_P_QN])
    kvsb_ref[...] = kv
    kvsbb_ref[...] = kv.astype(BF16)
    qn_ref[...] = _dot(u, w_ref[:, _P_QN:_P_KVN]).astype(BF16)
    kv = _dot(u, w_ref[:, _P_KVN:_P_WIN])
    kvn_ref[...] = kv
    kvnb_ref[...] = kv.astype(BF16)
    kv = _dot(u, w_ref[:, _P_WIN:_P_GATE])
    win_ref[...] = kv
    winb_ref[...] = kv.astype(BF16)
    gate_ref[...] = _sigmoid(_dot(u, w_ref[:, _P_GATE:_P_END]))


def _proj(x, mod4, w_proj, *, mod_base, per_token, tiles_per_seq):
    t, d = x.shape
    tm = min(TOKEN_TILE, t)
    row = lambda i: (i, 0)
    widths = [(H_SB * LANES, BF16), (2 * SB_W, F32), (2 * SB_W, BF16), (H_NSA * LANES, BF16),
              (4 * KV_W, F32), (4 * KV_W, BF16), (2 * KV_W, F32), (2 * KV_W, BF16), (LANES, F32)]
    return pl.pallas_call(
        functools.partial(_proj_kernel, mod_base=mod_base),
        grid=(t // tm,),
        in_specs=[pl.BlockSpec((tm, d), row), _mod_spec(per_token, tm, d, tiles_per_seq),
                  pl.BlockSpec(w_proj.shape, lambda i: (0, 0))],
        out_specs=[pl.BlockSpec((tm, w), row) for w, _ in widths],
        out_shape=[jax.ShapeDtypeStruct((t, w), dt) for w, dt in widths],
        compiler_params=_cparams(1),
        name="proj_in",
    )(x, mod4, w_proj)


def _compress_weights(w_cmp_k, w_cmp_v):
    n_half = CMP_BLOCK // CMP_STRIDE
    kinds = []
    for w in (w_cmp_k, w_cmp_v):
        w = w.reshape(n_half, CMP_STRIDE, HEAD_DIM, HEAD_DIM)
        blk = jnp.zeros((n_half, CMP_STRIDE, KV_W, KV_W), F32)
        for g in range(G_NSA):
            lo = g * HEAD_DIM
            blk = blk.at[:, :, lo:lo + HEAD_DIM, lo:lo + HEAD_DIM].set(w)
        kinds.append(jnp.concatenate([blk[0], blk[1]], axis=-1))
    return jnp.stack(kinds, axis=1).astype(BF16)


def _compress_kernel(*refs, n_x, rows):
    if len(refs) == 2 * n_x + 3:
        refs = refs[1:]
    w_ref, o_ref = refs[2 * n_x], refs[2 * n_x + 1]
    n_chunk = rows // CMP_STRIDE
    for kind in range(2):
        x_refs = refs[kind * n_x:(kind + 1) * n_x]
        acc = jnp.zeros((o_ref.shape[0], 2 * KV_W), F32)
        for i in range(CMP_STRIDE):
            parts = [xr[pl.ds(i, n_chunk, stride=CMP_STRIDE), :] for xr in x_refs]
            xi = parts[0] if n_x == 1 else jnp.concatenate(parts, axis=0)
            acc = acc + _dot(xi.astype(BF16), w_ref[i, kind])
        o_ref[:, kind * KV_W:(kind + 1) * KV_W] = acc[:, :KV_W]
        o_ref[:, (2 + kind) * KV_W:(3 + kind) * KV_W] = acc[:, KV_W:]


def _compress_prompt(kv_nsa, w_cmp, batch, seq):
    n_chunk = seq // CMP_STRIDE
    return pl.pallas_call(
        functools.partial(_compress_kernel, n_x=1, rows=seq),
        grid=(batch,),
        in_specs=[pl.BlockSpec((seq, KV_W), lambda b: (b, 0)),
                  pl.BlockSpec((seq, KV_W), lambda b: (b, 1)),
                  pl.BlockSpec(w_cmp.shape, lambda b: (0, 0, 0, 0))],
        out_specs=pl.BlockSpec((None, n_chunk, 4 * KV_W), lambda b: (b, 0, 0)),
        out_shape=jax.ShapeDtypeStruct((batch, n_chunk, 4 * KV_W), F32),
        compiler_params=_cparams(1),
        name="compress_prompt",
    )(kv_nsa, kv_nsa, w_cmp)


def _compress_pages(cache_nsa3, pages_flat, w_cmp, n_seq, n_pages, page):
    per_step = min(32, n_pages)
    n_steps = n_pages // per_step
    chunks = per_step * page // CMP_STRIDE

    def page_spec(k, kind):
        return pl.BlockSpec((None, page, KV_W),
                            lambda b, j, pt: (pt[b * n_pages + j * per_step + k], 0, kind))

    grid_spec = pltpu.PrefetchScalarGridSpec(
        num_scalar_prefetch=1,
        grid=(n_seq, n_steps),
        in_specs=[page_spec(k, kind) for kind in range(2) for k in range(per_step)]
        + [pl.BlockSpec(w_cmp.shape, lambda b, j, pt: (0, 0, 0, 0))],
        out_specs=pl.BlockSpec((None, chunks, 4 * KV_W), lambda b, j, pt: (b, j, 0)),
    )
    return pl.pallas_call(
        functools.partial(_compress_kernel, n_x=per_step, rows=page),
        grid_spec=grid_spec,
        out_shape=jax.ShapeDtypeStruct((n_seq, n_steps * chunks, 4 * KV_W), F32),
        compiler_params=_cparams(2),
        name="compress_pages",
    )(pages_flat, *([cache_nsa3] * (2 * per_step)), w_cmp)


def _col_stack(vals, rows):
    return jnp.concatenate([jnp.full((rows, 1), v, F32) for v in vals], axis=0)


def _slope_rows(g, nq):
    return _col_stack([2.0 ** -(g * HPG + h + 1) for h in range(HPG)], nq)


def _rep_rows(x, n):
    return jnp.concatenate([x] * n, axis=0)


def _masked_softmax(s, mask):
    s = jnp.where(mask, s, NEG)
    m = jnp.max(s, axis=-1, keepdims=True)
    e = jnp.where(mask, jnp.exp(s - m), 0.0)
    return e / jnp.maximum(jnp.sum(e, axis=-1, keepdims=True), 1e-30)


def _cmp_tables(ht):
    n_chunk = ht.shape[0]
    cmp = ht[:, :2 * KV_W] + pltpu.roll(ht[:, 2 * KV_W:], n_chunk - 1, 0)
    return cmp[:, :KV_W].astype(BF16), cmp[:, KV_W:].astype(BF16)


def _cmp_branch(qs, cmpk, cmpv, qpos_rows, slope_rows):
    n_chunk = cmpk.shape[0]
    s = _dot_nt(qs, cmpk)
    cend = lax.broadcasted_iota(jnp.int32, (1, n_chunk), 1) * CMP_STRIDE + (CMP_BLOCK - 1)
    cidx = lax.broadcasted_iota(jnp.int32, (1, n_chunk), 1)
    dist = qpos_rows - cend
    mask = jnp.logical_and(dist >= 0, cidx < n_chunk - 1)
    p = _masked_softmax(s - slope_rows * dist.astype(F32), mask)
    return p, _dot(p.astype(BF16), cmpv)


def _select_blocks(imp, m_blk, qpos_col, n_sb):
    nbp = m_blk.shape[1]
    imp_blk = sum(_dot(part, m_blk) for part in _split_bf16(imp, 3))
    bidx = lax.broadcasted_iota(jnp.int32, (1, nbp), 1)
    tb = jnp.right_shift(qpos_col, SEL_BLOCK.bit_length() - 1)
    valid = jnp.logical_and(bidx * SEL_BLOCK <= qpos_col, bidx < n_sb)
    forced = jnp.logical_or(bidx == 0, jnp.logical_or(bidx == tb, bidx == tb - 1))
    score = jnp.where(valid, imp_blk + jnp.where(forced, FORCE_BONUS, 0.0), -jnp.inf)
    cnt = jnp.zeros(score.shape, F32)
    for bp in range(n_sb):
        col = score[:, bp:bp + 1]
        beats = jnp.logical_or(col > score, jnp.logical_and(col == score, bidx > bp))
        cnt = cnt + jnp.where(beats, 1.0, 0.0)
    k_sel = min(N_SELECT, n_sb)
    return jnp.where(jnp.logical_and(cnt < k_sel, valid), 1.0, 0.0)


def _softmax_tile(qs, ktile, vtile, slope_rows, qpos_rows, kpos_cols, carry, *, window, extra=None):
    m, l, acc = carry
    dist = qpos_rows - kpos_cols
    s = _dot_nt(qs, ktile) - slope_rows * dist.astype(F32)
    mask = dist >= 0
    if window:
        mask = jnp.logical_and(mask, jnp.logical_and(dist < WINDOW, kpos_cols >= 0))
    if extra is not None:
        mask = jnp.logical_and(mask, extra)
    s = jnp.where(mask, s, NEG)
    m_new = jnp.maximum(m, jnp.max(s, axis=-1, keepdims=True))
    a = jnp.exp(m - m_new)
    p = jnp.where(mask, jnp.exp(s - m_new), 0.0)
    l = a * l + jnp.sum(p, axis=-1, keepdims=True)
    acc = a * acc + _dot(p.astype(BF16), vtile)
    return m_new, l, acc


def _softmax_init(rows):
    return (jnp.full((rows, 1), NEG, F32), jnp.zeros((rows, 1), F32), jnp.zeros((rows, LANES), F32))


def _softmax_finish(carry):
    _, l, acc = carry
    return acc / jnp.maximum(l, 1e-30)


def _sb_tile(qs, ktile, vtile, qpos_rows, kpos_cols, uu, carry):
    r, acc = carry
    z = _dot_nt(qs, ktile)
    causal = kpos_cols < qpos_rows
    softplus = jnp.maximum(z, 0.0) + jnp.log(1.0 + jnp.exp(-jnp.abs(z)))
    log_1mb = jnp.where(causal, -softplus, 0.0)
    hi, lo = _split_bf16(log_1mb, 2)
    between = _dot(jnp.concatenate([hi, lo], axis=1), uu)
    a = jnp.where(causal, jnp.exp((z - softplus) + between + r), 0.0)
    acc = acc + _dot(a.astype(BF16), vtile)
    r = r + jnp.sum(log_1mb, axis=-1, keepdims=True)
    return r, acc


def _gate_rows(gate, g, j, nq):
    cols = [gate[:, (g * HPG + h) * 3 + j:(g * HPG + h) * 3 + j + 1] for h in range(HPG)]
    return jnp.concatenate(cols, axis=0)


def _lane_lt(shape, n):
    return lax.broadcasted_iota(jnp.int32, shape, 1) < n


def _sb_prompt_kernel(q_ref, k_ref, v_ref, uu_ref, o_ref):
    qt = pl.program_id(1)
    nq = Q_TILE
    q0 = qt * nq
    qpos = q0 + lax.broadcasted_iota(jnp.int32, (nq, 1), 0)
    qpos_rows = _rep_rows(qpos, 2)
    kcol = lax.broadcasted_iota(jnp.int32, (1, KEY_TILE), 1)
    uu = uu_ref[...]
    for pair in range(H_SB // 2):
        lo = pair * LANES
        qs = jnp.concatenate([q_ref[:, (2 * pair) * LANES:(2 * pair + 1) * LANES],
                              q_ref[:, (2 * pair + 1) * LANES:(2 * pair + 2) * LANES]], axis=0)

        def body(i, carry, qs=qs, lo=lo):
            k0 = pl.multiple_of((qt - i) * KEY_TILE, KEY_TILE)
            ktile = k_ref[pl.ds(k0, KEY_TILE), lo:lo + LANES]
            vtile = v_ref[pl.ds(k0, KEY_TILE), lo:lo + LANES]
            return _sb_tile(qs, ktile, vtile, qpos_rows, k0 + kcol, uu, carry)

        _, acc = lax.fori_loop(0, qt + 1, body,
                               (jnp.zeros((2 * nq, 1), F32), jnp.zeros((2 * nq, LANES), F32)))
        o = jnp.where(_lane_lt((nq, LANES), HEAD_DIM), acc[:nq], acc[nq:])
        o_ref[:, lo:lo + LANES] = o.astype(BF16)


def _sb_prompt(q_sb_pad, kv_sb_bf, uu, batch, seq):
    nqt = seq // Q_TILE
    return pl.pallas_call(
        _sb_prompt_kernel,
        grid=(batch, nqt),
        in_specs=[pl.BlockSpec((Q_TILE, H_SB * LANES), lambda b, t: (b * nqt + t, 0)),
                  pl.BlockSpec((seq, SB_W), lambda b, t: (b, 0)),
                  pl.BlockSpec((seq, SB_W), lambda b, t: (b, 1)),
                  pl.BlockSpec(uu.shape, lambda b, t: (0, 0))],
        out_specs=pl.BlockSpec((Q_TILE, SB_W), lambda b, t: (b * nqt + t, 0)),
        out_shape=jax.ShapeDtypeStruct((batch * seq, SB_W), BF16),
        compiler_params=_cparams(2),
        name="sb_prompt",
    )(q_sb_pad, kv_sb_bf, kv_sb_bf, uu)


def _nsa_prompt_kernel(q_ref, gate_ref, ht_ref, slck_ref, slcv_ref, wink_ref, winv_ref,
                       mblk_ref, e_ref, o_ref, *, n_sb):
    qt = pl.program_id(1)
    nq = Q_TILE
    rows = HPG * nq
    q0 = qt * nq
    qpos = q0 + lax.broadcasted_iota(jnp.int32, (nq, 1), 0)
    qpos_rows = _rep_rows(qpos, HPG)
    kcol = lax.broadcasted_iota(jnp.int32, (1, KEY_TILE), 1)
    gate = gate_ref[...]
    cmpk, cmpv = _cmp_tables(ht_ref[...])
    outs = []
    for g in range(G_NSA):
        qs = jnp.concatenate([q_ref[:, (g * HPG + h) * LANES:(g * HPG + h + 1) * LANES]
                              for h in range(HPG)], axis=0)
        slopes = _slope_rows(g, nq)
        p, o_c = _cmp_branch(qs, cmpk, cmpv, qpos_rows, slopes)
        imp = p[0:nq] + p[nq:2 * nq] + p[2 * nq:3 * nq] + p[3 * nq:4 * nq]
        sel = _select_blocks(imp, mblk_ref[...], qpos, n_sb).astype(BF16)

        def slc_body(kt, carry, qs=qs, slopes=slopes, sel=sel):
            k0 = pl.multiple_of(kt * KEY_TILE, KEY_TILE)
            chosen = _rep_rows(_dot(sel, e_ref[kt]), HPG) > 0.5
            return _softmax_tile(qs, slck_ref[pl.ds(k0, KEY_TILE), :], slcv_ref[pl.ds(k0, KEY_TILE), :],
                                 slopes, qpos_rows, k0 + kcol, carry, window=False, extra=chosen)

        o_s = _softmax_finish(lax.fori_loop(0, qt + 1, slc_body, _softmax_init(rows)))

        def win_body(kt, carry, qs=qs, slopes=slopes):
            k0 = pl.multiple_of(kt * KEY_TILE, KEY_TILE)
            return _softmax_tile(qs, wink_ref[pl.ds(k0, KEY_TILE), :], winv_ref[pl.ds(k0, KEY_TILE), :],
                                 slopes, qpos_rows, k0 + kcol, carry, window=True)

        first = jnp.maximum(qt - WINDOW // KEY_TILE, 0)
        o_w = _softmax_finish(lax.fori_loop(first, qt + 1, win_body, _softmax_init(rows)))
        outs.append(_gate_rows(gate, g, 0, nq) * o_c + _gate_rows(gate, g, 1, nq) * o_s
                    + _gate_rows(gate, g, 2, nq) * o_w)
    for h in range(HPG):
        o = jnp.where(_lane_lt((nq, LANES), HEAD_DIM), outs[0][h * nq:(h + 1) * nq],
                      outs[1][h * nq:(h + 1) * nq])
        o_ref[:, h * LANES:(h + 1) * LANES] = o.astype(BF16)


def _nsa_prompt(q_n_pad, gate, ht, kv_nsa_bf, kv_win_bf, m_blk, e_tiles, batch, seq):
    nqt = seq // Q_TILE
    n_chunk = seq // CMP_STRIDE
    seq_blk = lambda j: pl.BlockSpec((seq, KV_W), lambda b, t: (b, j))
    return pl.pallas_call(
        functools.partial(_nsa_prompt_kernel, n_sb=seq // SEL_BLOCK),
        grid=(batch, nqt),
        in_specs=[pl.BlockSpec((Q_TILE, H_NSA * LANES), lambda b, t: (b * nqt + t, 0)),
                  pl.BlockSpec((Q_TILE, LANES), lambda b, t: (b * nqt + t, 0)),
                  pl.BlockSpec((None, n_chunk, 4 * KV_W), lambda b, t: (b, 0, 0)),
                  seq_blk(2), seq_blk(3), seq_blk(0), seq_blk(1),
                  pl.BlockSpec(m_blk.shape, lambda b, t: (0, 0)),
                  pl.BlockSpec(e_tiles.shape, lambda b, t: (0, 0, 0))],
        out_specs=pl.BlockSpec((Q_TILE, NSA_QW), lambda b, t: (b * nqt + t, 0)),
        out_shape=jax.ShapeDtypeStruct((batch * seq, NSA_QW), BF16),
        compiler_params=_cparams(2),
        name="nsa_prompt",
    )(q_n_pad, gate, ht, kv_nsa_bf, kv_nsa_bf, kv_win_bf, kv_win_bf, m_blk, e_tiles)


def _pad_keys(x):
    return jnp.concatenate([x, jnp.zeros((KEY_TILE - Q_PAD, x.shape[1]), F32)], axis=0).astype(BF16)


def _nsa_sample_local_kernel(q_ref, gate_ref, ht_ref, win_ref, neww_ref, mblk_ref,
                             part_ref, sel_ref, *, past, n_sb):
    nq = Q_PAD
    rows = HPG * nq
    qpos = past + lax.broadcasted_iota(jnp.int32, (nq, 1), 0)
    qpos_rows = _rep_rows(qpos, HPG)
    kcol = lax.broadcasted_iota(jnp.int32, (1, KEY_TILE), 1)
    gate = gate_ref[...]
    cmpk, cmpv = _cmp_tables(ht_ref[...])
    n_w = win_ref.shape[0]
    new_tile = _pad_keys(neww_ref[...])
    for g in range(G_NSA):
        qs = q_ref[g * rows:(g + 1) * rows, :]
        slopes = _slope_rows(g, nq)
        p, o_c = _cmp_branch(qs, cmpk, cmpv, qpos_rows, slopes)
        imp = p[0:nq] + p[nq:2 * nq] + p[2 * nq:3 * nq] + p[3 * nq:4 * nq]
        sel_ref[g * nq:(g + 1) * nq, :] = _select_blocks(imp, mblk_ref[...], qpos, n_sb)
        carry = _softmax_init(rows)
        for t in range(n_w // KEY_TILE):
            tile = win_ref[t * KEY_TILE:(t + 1) * KEY_TILE, :].astype(BF16)
            carry = _softmax_tile(qs, tile[:, :KV_W], tile[:, KV_W:], slopes, qpos_rows,
                                  (past - n_w + t * KEY_TILE) + kcol, carry, window=True)
        carry = _softmax_tile(qs, new_tile[:, :KV_W], new_tile[:, KV_W:], slopes, qpos_rows,
                              past + kcol, carry, window=True)
        o_w = _softmax_finish(carry)
        part_ref[g * rows:(g + 1) * rows, :] = (_gate_rows(gate, g, 0, nq) * o_c
                                                + _gate_rows(gate, g, 2, nq) * o_w)


def _nsa_sample_local(q_n_rows, gate8, ht, win_state, new_win8, m_blk, past, n_sb):
    n_seq = q_n_rows.shape[0]
    blk = lambda a: pl.BlockSpec((None,) + a.shape[1:], lambda b: (b,) + (0,) * (a.ndim - 1))
    nbp = m_blk.shape[1]
    return pl.pallas_call(
        functools.partial(_nsa_sample_local_kernel, past=past, n_sb=n_sb),
        grid=(n_seq,),
        in_specs=[blk(q_n_rows), blk(gate8), blk(ht), blk(win_state), blk(new_win8),
                  pl.BlockSpec(m_blk.shape, lambda b: (0, 0))],
        out_specs=[pl.BlockSpec((None, H_NSA * Q_PAD, LANES), lambda b: (b, 0, 0)),
                   pl.BlockSpec((None, G_NSA * Q_PAD, nbp), lambda b: (b, 0, 0))],
        out_shape=[jax.ShapeDtypeStruct((n_seq, H_NSA * Q_PAD, LANES), F32),
                   jax.ShapeDtypeStruct((n_seq, G_NSA * Q_PAD, nbp), F32)],
        compiler_params=_cparams(1),
        name="nsa_sample_local",
    )(q_n_rows, gate8, ht, win_state, new_win8, m_blk)


def _stream_sample_kernel(*refs, per_step, n_pages, past):
    pt_ref = refs[0]
    del pt_ref
    refs = refs[1:]
    sb_pages = refs[:per_step]
    nsa_pages = refs[per_step:2 * per_step]
    (qsb_ref, qn_ref, newsb_ref, newslc_ref, sel_ref, part_ref, gate_ref, e_ref, uu_ref,
     osb_ref, onsa_ref, sbr_ref, sbacc_ref, m_ref, l_ref, acc_ref) = refs[2 * per_step:]
    j = pl.program_id(1)
    nq = Q_PAD
    rows = HPG * nq
    n_pair = H_SB // 2
    qpos = past + lax.broadcasted_iota(jnp.int32, (nq, 1), 0)
    qpos2 = _rep_rows(qpos, 2)
    qpos4 = _rep_rows(qpos, HPG)
    kcol = lax.broadcasted_iota(jnp.int32, (1, KEY_TILE), 1)
    uu = uu_ref[...]

    def process(k_sb, v_sb, k_slc, v_slc, e_tile, kpos):
        for pair in range(n_pair):
            lo = pair * LANES
            qs = qsb_ref[pair * 2 * nq:(pair + 1) * 2 * nq, :]
            r, acc = _sb_tile(qs, k_sb[:, lo:lo + LANES], v_sb[:, lo:lo + LANES], qpos2, kpos, uu,
                              (sbr_ref[pair], sbacc_ref[pair]))
            sbr_ref[pair] = r
            sbacc_ref[pair] = acc
        for g in range(G_NSA):
            qs = qn_ref[g * rows:(g + 1) * rows, :]
            sel = sel_ref[g * nq:(g + 1) * nq, :].astype(BF16)
            chosen = _rep_rows(_dot(sel, e_tile), HPG) > 0.5
            m, l, acc = _softmax_tile(qs, k_slc, v_slc, _slope_rows(g, nq), qpos4, kpos,
                                      (m_ref[g], l_ref[g], acc_ref[g]), window=False, extra=chosen)
            m_ref[g] = m
            l_ref[g] = l
            acc_ref[g] = acc

    @pl.when(j == 0)
    def _():
        sbr_ref[...] = jnp.zeros(sbr_ref.shape, F32)
        sbacc_ref[...] = jnp.zeros(sbacc_ref.shape, F32)
        m_ref[...] = jnp.full(m_ref.shape, NEG, F32)
        l_ref[...] = jnp.zeros(l_ref.shape, F32)
        acc_ref[...] = jnp.zeros(acc_ref.shape, F32)
        new_sb = _pad_keys(newsb_ref[...])
        new_slc = _pad_keys(newslc_ref[...])
        process(new_sb[:, :SB_W], new_sb[:, SB_W:], new_slc[:, :KV_W], new_slc[:, KV_W:],
                e_ref[n_pages], past + kcol)

    for k in range(per_step):
        pg = n_pages - 1 - (j * per_step + k)
        sb = sb_pages[k][...].astype(BF16)
        slc = nsa_pages[k][...].astype(BF16)
        process(sb[:, :SB_W], sb[:, SB_W:], slc[:, :KV_W], slc[:, KV_W:], e_ref[pg],
                pg * KEY_TILE + kcol)

    @pl.when(j == pl.num_programs(1) - 1)
    def _():
        low = _lane_lt((nq, LANES), HEAD_DIM)
        for pair in range(n_pair):
            acc = sbacc_ref[pair]
            osb_ref[:, pair * LANES:(pair + 1) * LANES] = jnp.where(low, acc[:nq], acc[nq:]).astype(BF16)
        gate = gate_ref[...]
        outs = []
        for g in range(G_NSA):
            o_s = _softmax_finish((m_ref[g], l_ref[g], acc_ref[g]))
            outs.append(part_ref[g * rows:(g + 1) * rows, :] + _gate_rows(gate, g, 1, nq) * o_s)
        for h in range(HPG):
            o = jnp.where(low, outs[0][h * nq:(h + 1) * nq], outs[1][h * nq:(h + 1) * nq])
            onsa_ref[:, h * LANES:(h + 1) * LANES] = o.astype(BF16)


def _stream_sample(pages_flat, cache_sb3, cache_nsa3, q_sb_rows, q_n_rows, new_sb8, new_slc8,
                   sel, part, gate8, e_tiles, uu, n_pages, page, past):
    n_seq = q_sb_rows.shape[0]
    per_step = min(8, n_pages)
    n_steps = n_pages // per_step

    def page_idx(b, j, k, pt):
        return pt[b * n_pages + (n_pages - 1 - (j * per_step + k))]

    def sb_spec(k):
        return pl.BlockSpec((None, page, 2 * SB_W), lambda b, j, pt: (page_idx(b, j, k, pt), 0, 0))

    def nsa_spec(k):
        return pl.BlockSpec((None, page, 2 * KV_W), lambda b, j, pt: (page_idx(b, j, k, pt), 0, 1))

    seq_blk = lambda a: pl.BlockSpec((None,) + a.shape[1:],
                                     lambda b, j, pt: (b,) + (0,) * (a.ndim - 1))
    const = lambda a: pl.BlockSpec(a.shape, lambda b, j, pt: (0,) * a.ndim)
    grid_spec = pltpu.PrefetchScalarGridSpec(
        num_scalar_prefetch=1,
        grid=(n_seq, n_steps),
        in_specs=[sb_spec(k) for k in range(per_step)] + [nsa_spec(k) for k in range(per_step)]
        + [seq_blk(q_sb_rows), seq_blk(q_n_rows), seq_blk(new_sb8), seq_blk(new_slc8),
           seq_blk(sel), seq_blk(part), seq_blk(gate8), const(e_tiles), const(uu)],
        out_specs=[pl.BlockSpec((None, Q_PAD, SB_W), lambda b, j, pt: (b, 0, 0)),
                   pl.BlockSpec((None, Q_PAD, NSA_QW), lambda b, j, pt: (b, 0, 0))],
        scratch_shapes=[pltpu.VMEM((H_SB // 2, 2 * Q_PAD, 1), F32),
                        pltpu.VMEM((H_SB // 2, 2 * Q_PAD, LANES), F32),
                        pltpu.VMEM((G_NSA, HPG * Q_PAD, 1), F32),
                        pltpu.VMEM((G_NSA, HPG * Q_PAD, 1), F32),
                        pltpu.VMEM((G_NSA, HPG * Q_PAD, LANES), F32)],
    )
    return pl.pallas_call(
        functools.partial(_stream_sample_kernel, per_step=per_step, n_pages=n_pages, past=past),
        grid_spec=grid_spec,
        out_shape=[jax.ShapeDtypeStruct((n_seq, Q_PAD, SB_W), BF16),
                   jax.ShapeDtypeStruct((n_seq, Q_PAD, NSA_QW), BF16)],
        compiler_params=_cparams(2),
        name="stream_sample",
    )(pages_flat, *([cache_sb3] * per_step), *([cache_nsa3] * per_step), q_sb_rows, q_n_rows,
      new_sb8, new_slc8, sel, part, gate8, e_tiles, uu)


def _block_sum_matrix(n_chunk_pad, n_c, n_sb, nbp):
    per = SEL_BLOCK // CMP_STRIDE
    c = jnp.arange(n_chunk_pad)[:, None]
    b = jnp.arange(nbp)[None, :]
    own = (c // per == b)
    nxt = jnp.logical_and((c + 1) // per == b, c + 1 < n_sb * per)
    m = (own.astype(F32) + nxt.astype(F32)) * (c < n_c) * (b < n_sb)
    return m.astype(BF16)


def _expand_tiles(n_tiles, nbp):
    t = jnp.arange(n_tiles)[:, None, None]
    b = jnp.arange(nbp)[None, :, None]
    k = jnp.arange(KEY_TILE)[None, None, :]
    return ((t * KEY_TILE + k) // SEL_BLOCK == b).astype(BF16)


def _suffix_matrix():
    j = jnp.arange(KEY_TILE)[:, None]
    s = jnp.arange(KEY_TILE)[None, :]
    u = (j > s).astype(BF16)
    return jnp.concatenate([u, u], axis=0)


def _rows_per_seq(x, n_seq, n_new, slots):
    x = x.reshape(n_seq, n_new, slots, LANES)
    x = jnp.pad(x, ((0, 0), (0, Q_PAD - n_new), (0, 0), (0, 0)))
    return x.transpose(0, 2, 1, 3).reshape(n_seq, slots * Q_PAD, LANES)


def _pad_new(x, n_seq, n_new):
    x = x.reshape(n_seq, n_new, x.shape[-1])
    return jnp.pad(x, ((0, 0), (0, Q_PAD - n_new), (0, 0)))


def kernel(x_prompt, x_sample, cache_sb_kv, cache_nsa_kv, state_win_kv, page_table, c_prompt, c_sample,
           w_ada, b_ada, w_ff1_up, w_ff1_down, w_in, w_cmp_k, w_cmp_v, w_o, w_ff2_up, w_ff2_down,
           ln_g, ln_b):
    batch, seq, d = x_prompt.shape
    n_seq, n_new, _ = x_sample.shape
    depth, n_pool, page = cache_sb_kv.shape[:3]
    n_pages = page_table.shape[1]
    past = n_pages * page
    n_w = state_win_kv.shape[2]
    alpha = (2.0 * depth) ** 0.25
    assert page == KEY_TILE and seq % TOKEN_TILE == 0 and n_new <= Q_PAD

    uu = _suffix_matrix()
    n_chunk_p = seq // CMP_STRIDE
    n_sb_p = seq // SEL_BLOCK
    m_blk_p = _block_sum_matrix(n_chunk_p, n_chunk_p - 1, n_sb_p, LANES * pl.cdiv(n_sb_p, LANES))
    e_p = _expand_tiles(seq // KEY_TILE, m_blk_p.shape[1])
    t_all = past + n_new
    n_sb_s = pl.cdiv(t_all, SEL_BLOCK)
    n_chunk_s = past // CMP_STRIDE
    m_blk_s = _block_sum_matrix(n_chunk_s, t_all // CMP_STRIDE - 1, n_sb_s,
                                LANES * pl.cdiv(n_sb_s, LANES))
    e_s = _expand_tiles(n_pages + 1, m_blk_s.shape[1])

    xp = x_prompt.reshape(batch * seq, d)
    xs = x_sample.reshape(n_seq * n_new, d)
    c_all = jnp.concatenate([c_prompt, c_sample], axis=0)
    cache_sb3 = cache_sb_kv.reshape(depth * n_pool, page, 2 * SB_W)
    cache_nsa3 = cache_nsa_kv.reshape(depth * n_pool, page, 4 * KV_W)
    tiles_per_seq = seq // TOKEN_TILE
    states = [[] for _ in range(6)]

    for layer in range(depth):
        mod = _ada(c_all, w_ada[layer], b_ada[layer]).reshape(batch + n_seq, N_MOD, d)
        mod_p = mod[:batch].transpose(1, 0, 2)[:, :, None, :]
        mod_s = jnp.repeat(mod[batch:], n_new, axis=0).transpose(1, 0, 2)[:, None, :, :]
        w_up1, w_dn1 = w_ff1_up[layer].astype(BF16), w_ff1_down[layer].astype(BF16)
        w_up2, w_dn2 = w_ff2_up[layer].astype(BF16), w_ff2_down[layer].astype(BF16)
        w_proj = _proj_weights(w_in[layer])
        w_cmp = _compress_weights(w_cmp_k[layer], w_cmp_v[layer])
        w_osb = w_o[layer, :SB_W].astype(BF16)
        w_onsa = (w_o[layer, SB_W:].reshape(G_NSA, HPG, HEAD_DIM, d).transpose(1, 0, 2, 3)
                  .reshape(NSA_QW, d).astype(BF16))
        lng, lnb = ln_g[layer], ln_b[layer]
        pages_flat = (page_table + layer * n_pool).reshape(-1).astype(jnp.int32)

        kw_p = dict(per_token=False, tiles_per_seq=tiles_per_seq)
        x1 = _ffn(xp, mod_p, w_up1, w_dn1, lng[0:1], lnb[0:1], mod_base=0, alpha=alpha, **kw_p)
        (q_sb, kv_sb, kv_sb_bf, q_n, kv_nsa, kv_nsa_bf, kv_win, kv_win_bf, gate) = _proj(
            x1, mod_p, w_proj, mod_base=3, **kw_p)
        ht = _compress_prompt(kv_nsa, w_cmp, batch, seq)
        o_sb = _sb_prompt(q_sb, kv_sb_bf, uu, batch, seq)
        o_nsa = _nsa_prompt(q_n, gate, ht, kv_nsa_bf, kv_win_bf, m_blk_p, e_p, batch, seq)
        xp = _ffn(x1, mod_p, w_up2, w_dn2, lng[1:3], lnb[1:3], mod_base=6, alpha=alpha,
                  oproj=(o_sb, o_nsa, w_osb, w_onsa), **kw_p)
        n_keep = min(WINDOW, seq)
        states[0].append(kv_sb.reshape(batch, seq, 2, H_SB, HEAD_DIM))
        states[1].append(kv_nsa.reshape(batch, seq, 4, G_NSA, HEAD_DIM))
        states[2].append(kv_win.reshape(batch, seq, 2, G_NSA, HEAD_DIM)[:, seq - n_keep:])

        kw_s = dict(per_token=True, tiles_per_seq=1)
        s1 = _ffn(xs, mod_s, w_up1, w_dn1, lng[0:1], lnb[0:1], mod_base=0, alpha=alpha, **kw_s)
        (q_sb, kv_sb, _, q_n, kv_nsa, _, kv_win, _, gate) = _proj(s1, mod_s, w_proj, mod_base=3, **kw_s)
        ht = _compress_pages(cache_nsa3, pages_flat, w_cmp, n_seq, n_pages, page)
        q_n_rows = _rows_per_seq(q_n, n_seq, n_new, H_NSA)
        q_sb_rows = _rows_per_seq(q_sb, n_seq, n_new, H_SB)
        gate8 = _pad_new(gate, n_seq, n_new)
        win_state = state_win_kv[layer].reshape(n_seq, n_w, 2 * KV_W)
        part, sel = _nsa_sample_local(q_n_rows, gate8, ht, win_state, _pad_new(kv_win, n_seq, n_new),
                                      m_blk_s, past, n_sb_s)
        o_sb, o_nsa = _stream_sample(pages_flat, cache_sb3, cache_nsa3, q_sb_rows, q_n_rows,
                                     _pad_new(kv_sb, n_seq, n_new),
                                     _pad_new(kv_nsa[:, 2 * KV_W:], n_seq, n_new),
                                     sel, part, gate8, e_s, uu, n_pages, page, past)
        o_sb = o_sb[:, :n_new].reshape(n_seq * n_new, SB_W)
        o_nsa = o_nsa[:, :n_new].reshape(n_seq * n_new, NSA_QW)
        xs = _ffn(s1, mod_s, w_up2, w_dn2, lng[1:3], lnb[1:3], mod_base=6, alpha=alpha,
                  oproj=(o_sb, o_nsa, w_osb, w_onsa), **kw_s)
        win_all = jnp.concatenate([state_win_kv[layer],
                                   kv_win.reshape(n_seq, n_new, 2, G_NSA, HEAD_DIM)], axis=1)
        states[3].append(kv_sb.reshape(n_seq, n_new, 2, H_SB, HEAD_DIM))
        states[4].append(kv_nsa.reshape(n_seq, n_new, 4, G_NSA, HEAD_DIM))
        states[5].append(win_all[:, n_new:])

    return (xp.reshape(batch, seq, d), xs.reshape(n_seq, n_new, d)) + tuple(jnp.stack(s) for s in states)
```

```python
import functools

import jax
import jax.numpy as jnp
from jax import lax
from jax.experimental import pallas as pl
from jax.experimental.pallas import tpu as pltpu

F32 = jnp.float32
BF16 = jnp.bfloat16

HEAD_DIM = 64
H_SB = 8
H_NSA = 8
G_NSA = 2
HPG = H_NSA // G_NSA
CMP_BLOCK = 32
CMP_STRIDE = 16
SEL_BLOCK = 64
N_SELECT = 16
WINDOW = 512
N_MOD = 9
LN_EPS = 1e-5
FORCE_BONUS = 1.0e4
NEG = -1.0e30

SB_W = H_SB * HEAD_DIM
NSA_QW = H_NSA * HEAD_DIM
KV_W = G_NSA * HEAD_DIM
OFF_NSA_Q = 3 * SB_W
OFF_NSA_KV = OFF_NSA_Q + NSA_QW
OFF_WIN = OFF_NSA_KV + 4 * KV_W
OFF_GATE = OFF_WIN + 2 * KV_W
N_IN = OFF_GATE + 3 * H_NSA

LANES = 128
KEY_TILE = 128
VMEM_LIMIT = 56 * 1024 * 1024

TOKEN_TILE = 512
Q_TILE = 128
FF_CHUNK = 256
Q_PAD = 8
SB_EXIT = -88.0


def _cparams(n_axes):
    return pltpu.CompilerParams(dimension_semantics=("arbitrary",) * n_axes,
                                vmem_limit_bytes=VMEM_LIMIT)


def _dot(a, b):
    return jnp.dot(a, b, preferred_element_type=F32)


def _dot_nt(a, b):
    return lax.dot_general(a, b, (((1,), (1,)), ((), ())), preferred_element_type=F32)


def _split_bf16(x, parts):
    out = []
    r = x
    for _ in range(parts):
        h = r.astype(BF16)
        out.append(h)
        r = r - h.astype(F32)
    return out


def _sigmoid(x):
    return 1.0 / (1.0 + jnp.exp(-x))


def _layer_norm(y, g, b):
    mu = jnp.mean(y, axis=-1, keepdims=True)
    d = y - mu
    var = jnp.mean(d * d, axis=-1, keepdims=True)
    return d * lax.rsqrt(var + LN_EPS) * g + b


def _ada_kernel(c_ref, w_ref, b_ref, o_ref):
    c = c_ref[...]
    a = c * _sigmoid(c)
    a_hi, a_lo = _split_bf16(a, 2)
    w_hi, w_lo = _split_bf16(w_ref[...], 2)
    o_ref[...] = _dot(a_hi, w_hi) + (_dot(a_hi, w_lo) + _dot(a_lo, w_hi)) + b_ref[...]


def _ada(c, w, b):
    n, d = c.shape
    n_out = w.shape[1]
    bn = 1024
    return pl.pallas_call(
        _ada_kernel,
        grid=(n_out // bn,),
        in_specs=[pl.BlockSpec((n, d), lambda j: (0, 0)),
                  pl.BlockSpec((d, bn), lambda j: (0, j)),
                  pl.BlockSpec((1, bn), lambda j: (0, j))],
        out_specs=pl.BlockSpec((n, bn), lambda j: (0, j)),
        out_shape=jax.ShapeDtypeStruct((n, n_out), F32),
        compiler_params=_cparams(1),
        name="ada",
    )(c, w, b.reshape(1, n_out))


def _mod_spec(per_token, tm, d, tiles_per_seq):
    if per_token:
        return pl.BlockSpec((N_MOD, None, tm, d), lambda i: (0, 0, i, 0))
    return pl.BlockSpec((N_MOD, None, 1, d), lambda i: (0, i // tiles_per_seq, 0, 0))


def _ffn_kernel(*refs, mod_base, d_ff, alpha, fuse_oproj):
    if fuse_oproj:
        (x_ref, osb_ref, onsa_ref, mod_ref, wosb_ref, wonsa_ref, wup_ref, wdn_ref,
         lng_ref, lnb_ref, o_ref, acc_ref) = refs
        mixed = _dot(osb_ref[...], wosb_ref[...]) + _dot(onsa_ref[...], wonsa_ref[...])
        x = _layer_norm(alpha * x_ref[...] + mod_ref[mod_base - 1] * mixed,
                        lng_ref[0:1, :], lnb_ref[0:1, :])
        ln_row = 1
    else:
        x_ref, mod_ref, wup_ref, wdn_ref, lng_ref, lnb_ref, o_ref, acc_ref = refs
        x = x_ref[...]
        ln_row = 0
    u = (x * (1.0 + mod_ref[mod_base + 1]) + mod_ref[mod_base]).astype(BF16)
    for c in range(d_ff // FF_CHUNK):
        lo = c * FF_CHUNK
        g = _dot(u, wup_ref[:, lo:lo + FF_CHUNK])
        v = _dot(u, wup_ref[:, d_ff + lo:d_ff + lo + FF_CHUNK])
        act = (g * _sigmoid(g) * v).astype(BF16)
        part = _dot(act, wdn_ref[lo:lo + FF_CHUNK, :])
        if c == 0:
            acc_ref[...] = part
        else:
            acc_ref[...] += part
    y = alpha * x + 0.5 * mod_ref[mod_base + 2] * acc_ref[...]
    o_ref[...] = _layer_norm(y, lng_ref[ln_row:ln_row + 1, :], lnb_ref[ln_row:ln_row + 1, :])


def _ffn(x, mod4, w_up, w_down, ln_g2, ln_b2, *, mod_base, alpha, per_token, tiles_per_seq,
         oproj=None):
    t, d = x.shape
    d_ff = w_down.shape[0]
    tm = min(TOKEN_TILE, t)
    row = lambda i: (i, 0)
    const = lambda i: (0, 0)
    in_specs = [pl.BlockSpec((tm, d), row)]
    args = [x]
    if oproj is not None:
        o_sb, o_nsa, w_osb, w_onsa = oproj
        in_specs += [pl.BlockSpec((tm, SB_W), row), pl.BlockSpec((tm, NSA_QW), row)]
        args += [o_sb, o_nsa]
    in_specs.append(_mod_spec(per_token, tm, d, tiles_per_seq))
    args.append(mod4)
    if oproj is not None:
        in_specs += [pl.BlockSpec((SB_W, d), const), pl.BlockSpec((NSA_QW, d), const)]
        args += [w_osb, w_onsa]
    in_specs += [pl.BlockSpec((d, 2 * d_ff), const), pl.BlockSpec((d_ff, d), const),
                 pl.BlockSpec(ln_g2.shape, const), pl.BlockSpec(ln_b2.shape, const)]
    args += [w_up, w_down, ln_g2, ln_b2]
    return pl.pallas_call(
        functools.partial(_ffn_kernel, mod_base=mod_base, d_ff=d_ff, alpha=alpha,
                          fuse_oproj=oproj is not None),
        grid=(t // tm,),
        in_specs=in_specs,
        out_specs=pl.BlockSpec((tm, d), row),
        out_shape=jax.ShapeDtypeStruct((t, d), F32),
        scratch_shapes=[pltpu.VMEM((tm, d), F32)],
        compiler_params=_cparams(1),
        name="ffn_oproj" if oproj is not None else "ffn",
    )(*args)


_P_QSB = 0
_P_QN = _P_QSB + H_SB * LANES
_P_GATE = _P_QN + H_NSA * LANES
_P_KVSB = _P_GATE + LANES
_P_KVN = _P_KVSB + 2 * SB_W
_P_WIN = _P_KVN + 4 * KV_W
_P_END = _P_WIN + 2 * KV_W
_KV_ROWS = _P_END - _P_KVSB


def _proj_weights(w_in):
    d = w_in.shape[0]
    scale = HEAD_DIM ** -0.5
    q_sb = (w_in[:, :SB_W] * scale).reshape(d, H_SB, HEAD_DIM)
    q_sb_pad = jnp.zeros((d, H_SB, LANES), F32)
    for h in range(H_SB):
        o = (h % 2) * HEAD_DIM
        q_sb_pad = q_sb_pad.at[:, h, o:o + HEAD_DIM].set(q_sb[:, h])
    q_n = (w_in[:, OFF_NSA_Q:OFF_NSA_KV] * scale).reshape(d, H_NSA, HEAD_DIM)
    q_n_pad = jnp.zeros((d, H_NSA, LANES), F32)
    for h in range(H_NSA):
        o = (h // HPG) * HEAD_DIM
        q_n_pad = q_n_pad.at[:, h, o:o + HEAD_DIM].set(q_n[:, h])
    gate = jnp.pad(w_in[:, OFF_GATE:], ((0, 0), (0, LANES - 3 * H_NSA)))
    w = jnp.concatenate([q_sb_pad.reshape(d, -1), q_n_pad.reshape(d, -1), gate, w_in[:, SB_W:3 * SB_W],
                         w_in[:, OFF_NSA_KV:OFF_WIN], w_in[:, OFF_WIN:OFF_GATE]], axis=1)
    return w.astype(BF16)


def _proj_rows(u, w_ref, qsb_ref, qn_ref, gate_ref):
    qsb_ref[...] = _dot(u, w_ref[:, _P_QSB:_P_QN]).astype(BF16)
    qn_ref[...] = _dot(u, w_ref[:, _P_QN:_P_GATE]).astype(BF16)
    gate_ref[...] = _sigmoid(_dot(u, w_ref[:, _P_GATE:_P_KVSB]))


def _proj_sample_kernel(x_ref, mod_ref, w_ref, qsb_ref, qn_ref, gate_ref, kvsb_ref, kvn_ref, win_ref,
                        *, mod_base):
    u = (x_ref[...] * (1.0 + mod_ref[mod_base + 1]) + mod_ref[mod_base]).astype(BF16)
    _proj_rows(u, w_ref, qsb_ref, qn_ref, gate_ref)
    kvsb_ref[...] = _dot(u, w_ref[:, _P_KVSB:_P_KVN])
    kvn_ref[...] = _dot(u, w_ref[:, _P_KVN:_P_WIN])
    win_ref[...] = _dot(u, w_ref[:, _P_WIN:_P_END])


def _proj_sample(x, mod4, w_proj, *, mod_base):
    t, d = x.shape
    row = lambda i: (i, 0)
    widths = [(H_SB * LANES, BF16), (H_NSA * LANES, BF16), (LANES, F32), (2 * SB_W, F32),
              (4 * KV_W, F32), (2 * KV_W, F32)]
    return pl.pallas_call(
        functools.partial(_proj_sample_kernel, mod_base=mod_base),
        grid=(1,),
        in_specs=[pl.BlockSpec((t, d), row), _mod_spec(True, t, d, 1),
                  pl.BlockSpec(w_proj.shape, lambda i: (0, 0))],
        out_specs=[pl.BlockSpec((t, w), row) for w, _ in widths],
        out_shape=[jax.ShapeDtypeStruct((t, w), dt) for w, dt in widths],
        compiler_params=_cparams(1),
        name="proj_sample",
    )(x, mod4, w_proj)


def _proj_prompt_kernel(x_ref, mod_ref, w_ref, wt_ref, qsb_ref, qn_ref, gate_ref, cmp_ref,
                        sbt_ref, nsat_ref, wint_ref, sbtile_ref, slctile_ref, wintile_ref, *, mod_base):
    u = (x_ref[...] * (1.0 + mod_ref[mod_base + 1]) + mod_ref[mod_base]).astype(BF16)
    _proj_rows(u, w_ref, qsb_ref, qn_ref, gate_ref)
    cmp_ref[...] = _dot(u, w_ref[:, _P_KVN:_P_KVN + 2 * KV_W])
    kvt = _dot_nt(wt_ref[...], u)
    n_sb, n_nsa = 2 * SB_W, 4 * KV_W
    sbt_ref[...] = kvt[:n_sb]
    nsat_ref[...] = kvt[n_sb:n_sb + n_nsa]
    wint_ref[...] = kvt[n_sb + n_nsa:]
    kvb = kvt.astype(BF16)
    for j in range(sbtile_ref.shape[0]):
        cols = slice(j * KEY_TILE, (j + 1) * KEY_TILE)
        sbtile_ref[j] = kvb[:n_sb, cols]
        slctile_ref[j] = kvb[n_sb + 2 * KV_W:n_sb + n_nsa, cols]
        wintile_ref[j] = kvb[n_sb + n_nsa:, cols]


def _proj_prompt(x, mod4, w_proj, w_kv_t, batch, seq, *, mod_base):
    t, d = x.shape
    tm = TOKEN_TILE
    tps = seq // tm
    kt = tm // KEY_TILE
    nkt = seq // KEY_TILE
    row = lambda i: (i, 0)
    tr = lambda i: (i // tps, 0, i % tps)
    tile = lambda i: (i // tps, i % tps, 0, 0)
    rows = [(H_SB * LANES, BF16), (H_NSA * LANES, BF16), (LANES, F32), (2 * KV_W, F32)]
    trans = [2 * SB_W, 4 * KV_W, 2 * KV_W]
    tiles = [2 * SB_W, 2 * KV_W, 2 * KV_W]
    return pl.pallas_call(
        functools.partial(_proj_prompt_kernel, mod_base=mod_base),
        grid=(t // tm,),
        in_specs=[pl.BlockSpec((tm, d), row), _mod_spec(False, tm, d, tps),
                  pl.BlockSpec(w_proj.shape, lambda i: (0, 0)),
                  pl.BlockSpec(w_kv_t.shape, lambda i: (0, 0))],
        out_specs=[pl.BlockSpec((tm, w), row) for w, _ in rows]
        + [pl.BlockSpec((None, r, tm), tr) for r in trans]
        + [pl.BlockSpec((None, kt, r, KEY_TILE), tile) for r in tiles],
        out_shape=[jax.ShapeDtypeStruct((t, w), dt) for w, dt in rows]
        + [jax.ShapeDtypeStruct((batch, r, seq), F32) for r in trans]
        + [jax.ShapeDtypeStruct((batch, nkt, r, KEY_TILE), BF16) for r in tiles],
        compiler_params=_cparams(1),
        name="proj_prompt",
    )(x, mod4, w_proj, w_kv_t)


def _compress_weights(w_cmp_k, w_cmp_v):
    n_half = CMP_BLOCK // CMP_STRIDE
    kinds = []
    for w in (w_cmp_k, w_cmp_v):
        w = w.reshape(n_half, CMP_STRIDE, HEAD_DIM, HEAD_DIM)
        blk = jnp.zeros((n_half, CMP_STRIDE, KV_W, KV_W), F32)
        for g in range(G_NSA):
            lo = g * HEAD_DIM
            blk = blk.at[:, :, lo:lo + HEAD_DIM, lo:lo + HEAD_DIM].set(w)
        kinds.append(jnp.concatenate([blk[0], blk[1]], axis=-1))
    return jnp.stack(kinds, axis=1).astype(BF16)


def _compress_rows(xk_ref, xv_ref, w_ref, o_ref):
    n_chunk = o_ref.shape[0]
    for kind, x_ref in enumerate((xk_ref, xv_ref)):
        acc = jnp.zeros((n_chunk, 2 * KV_W), F32)
        for i in range(CMP_STRIDE):
            xi = x_ref[pl.ds(i, n_chunk, stride=CMP_STRIDE), :]
            acc = acc + _dot(xi.astype(BF16), w_ref[i, kind])
        o_ref[:, kind * KV_W:(kind + 1) * KV_W] = acc[:, :KV_W]
        o_ref[:, (2 + kind) * KV_W:(3 + kind) * KV_W] = acc[:, KV_W:]


def _compress_prompt_kernel(xk_ref, xv_ref, w_ref, o_ref):
    _compress_rows(xk_ref, xv_ref, w_ref, o_ref)


def _compress_pages_kernel(*refs, per_step):
    page_refs = refs[1:1 + per_step]
    w_ref, o_ref, xk_ref, xv_ref = refs[1 + per_step:]
    for k, p_ref in enumerate(page_refs):
        rows = slice(k * KEY_TILE, (k + 1) * KEY_TILE)
        xk_ref[rows, :] = p_ref[:KV_W, :].T
        xv_ref[rows, :] = p_ref[KV_W:, :].T
    _compress_rows(xk_ref, xv_ref, w_ref, o_ref)


def _compress_prompt(kv_nsa, w_cmp, batch, seq):
    n_chunk = seq // CMP_STRIDE
    return pl.pallas_call(
        _compress_prompt_kernel,
        grid=(batch,),
        in_specs=[pl.BlockSpec((seq, KV_W), lambda b: (b, 0)),
                  pl.BlockSpec((seq, KV_W), lambda b: (b, 1)),
                  pl.BlockSpec(w_cmp.shape, lambda b: (0, 0, 0, 0))],
        out_specs=pl.BlockSpec((None, n_chunk, 4 * KV_W), lambda b: (b, 0, 0)),
        out_shape=jax.ShapeDtypeStruct((batch, n_chunk, 4 * KV_W), F32),
        compiler_params=_cparams(1),
        name="compress_prompt",
    )(kv_nsa, kv_nsa, w_cmp)


def _compress_pages(cache_nsa_t, pages_flat, w_cmp, n_seq, n_pages):
    per_step = min(32, n_pages)
    n_steps = n_pages // per_step
    chunks = per_step * KEY_TILE // CMP_STRIDE

    def page_spec(k):
        return pl.BlockSpec((None, 2 * KV_W, KEY_TILE),
                            lambda b, j, pt: (pt[b * n_pages + j * per_step + k], 0, 0))

    grid_spec = pltpu.PrefetchScalarGridSpec(
        num_scalar_prefetch=1,
        grid=(n_seq, n_steps),
        in_specs=[page_spec(k) for k in range(per_step)]
        + [pl.BlockSpec(w_cmp.shape, lambda b, j, pt: (0, 0, 0, 0))],
        out_specs=pl.BlockSpec((None, chunks, 4 * KV_W), lambda b, j, pt: (b, j, 0)),
        scratch_shapes=[pltpu.VMEM((per_step * KEY_TILE, KV_W), F32),
                        pltpu.VMEM((per_step * KEY_TILE, KV_W), F32)],
    )
    return pl.pallas_call(
        functools.partial(_compress_pages_kernel, per_step=per_step),
        grid_spec=grid_spec,
        out_shape=jax.ShapeDtypeStruct((n_seq, n_steps * chunks, 4 * KV_W), F32),
        compiler_params=_cparams(2),
        name="compress_pages",
    )(pages_flat, *([cache_nsa_t] * per_step), w_cmp)


def _col_stack(vals, rows):
    return jnp.concatenate([jnp.full((rows, 1), v, F32) for v in vals], axis=0)


def _slope_rows(g, nq):
    return _col_stack([2.0 ** -(g * HPG + h + 1) for h in range(HPG)], nq)


def _rep_rows(x, n):
    return jnp.concatenate([x] * n, axis=0)


def _masked_softmax(s, mask):
    s = jnp.where(mask, s, NEG)
    m = jnp.max(s, axis=-1, keepdims=True)
    e = jnp.where(mask, jnp.exp(s - m), 0.0)
    return e / jnp.maximum(jnp.sum(e, axis=-1, keepdims=True), 1e-30)


def _cmp_tables(ht):
    n_chunk = ht.shape[0]
    cmp = ht[:, :2 * KV_W] + pltpu.roll(ht[:, 2 * KV_W:], n_chunk - 1, 0)
    return cmp[:, :KV_W].astype(BF16), cmp[:, KV_W:].astype(BF16)


def _cmp_branch(qs, cmpk, cmpv, qpos_rows, slope_rows):
    n_chunk = cmpk.shape[0]
    s = _dot_nt(qs, cmpk)
    cend = lax.broadcasted_iota(jnp.int32, (1, n_chunk), 1) * CMP_STRIDE + (CMP_BLOCK - 1)
    cidx = lax.broadcasted_iota(jnp.int32, (1, n_chunk), 1)
    dist = qpos_rows - cend
    mask = jnp.logical_and(dist >= 0, cidx < n_chunk - 1)
    p = _masked_softmax(s - slope_rows * dist.astype(F32), mask)
    return p, _dot(p.astype(BF16), cmpv)


def _select_blocks(imp, m_blk, qpos_col, n_sb, *, blocks_on_sublanes):
    nbp = m_blk.shape[1]
    imp_blk = sum(_dot(part, m_blk) for part in _split_bf16(imp, 3))
    bidx = lax.broadcasted_iota(jnp.int32, (1, nbp), 1)
    tb = jnp.right_shift(qpos_col, SEL_BLOCK.bit_length() - 1)
    valid = jnp.logical_and(bidx * SEL_BLOCK <= qpos_col, bidx < n_sb)
    forced = jnp.logical_or(bidx == 0, jnp.logical_or(bidx == tb, bidx == tb - 1))
    score = jnp.where(valid, imp_blk + jnp.where(forced, FORCE_BONUS, 0.0), -jnp.inf)
    k_sel = min(N_SELECT, n_sb)
    if blocks_on_sublanes:
        st = score.T
        ridx = lax.broadcasted_iota(jnp.int32, (nbp, 1), 0)
        cnt = jnp.zeros(st.shape, F32)
        for bp in range(n_sb):
            row = st[bp:bp + 1, :]
            beats = jnp.logical_or(row > st, jnp.logical_and(row == st, ridx > bp))
            cnt = cnt + jnp.where(beats, 1.0, 0.0)
        few = jnp.where(cnt < k_sel, 1.0, 0.0).T
    else:
        cnt = jnp.zeros(score.shape, F32)
        for bp in range(n_sb):
            col = score[:, bp:bp + 1]
            beats = jnp.logical_or(col > score, jnp.logical_and(col == score, bidx > bp))
            cnt = cnt + jnp.where(beats, 1.0, 0.0)
        few = jnp.where(cnt < k_sel, 1.0, 0.0)
    return jnp.where(valid, few, 0.0)


def _softmax_tiles(jobs, *, window):
    logits = [_dot(j[0], j[1]) for j in jobs]
    picked = [None if j[7] is None else _dot(j[7], j[8]) for j in jobs]
    mids = []
    for (qs, _, _, slope_rows, qpos_rows, kpos_cols, (m, l, _), _, _), s, pk in zip(jobs, logits, picked):
        dist = qpos_rows - kpos_cols
        s = s - slope_rows * dist.astype(F32)
        mask = dist >= 0
        if window:
            mask = jnp.logical_and(mask, jnp.logical_and(dist < WINDOW, kpos_cols >= 0))
        if pk is not None:
            mask = jnp.logical_and(mask, _rep_rows(pk, qs.shape[0] // pk.shape[0]) > 0.5)
        s = jnp.where(mask, s, NEG)
        m_new = jnp.maximum(m, jnp.max(s, axis=-1, keepdims=True))
        a = jnp.exp(m - m_new)
        p = jnp.where(mask, jnp.exp(s - m_new), 0.0)
        mids.append((m_new, a * l + jnp.sum(p, axis=-1, keepdims=True), a, p.astype(BF16)))
    return [(m_new, l, a * j[6][2] + _dot_nt(p, j[2])) for (m_new, l, a, p), j in zip(mids, jobs)]


def _softmax_init(rows):
    return (jnp.full((rows, 1), NEG, F32), jnp.zeros((rows, 1), F32), jnp.zeros((rows, LANES), F32))


def _softmax_finish(carry):
    _, l, acc = carry
    return acc / jnp.maximum(l, 1e-30)


def _sb_tiles(jobs, uu):
    logits = [[_dot(qs, kt) for kt, _, _ in tiles] for qs, _, _, tiles in jobs]
    mids = []
    for (_, qpos_rows, _, tiles), zs in zip(jobs, logits):
        row = []
        for z, (_, _, kpos_cols) in zip(zs, tiles):
            causal = kpos_cols < qpos_rows
            softplus = jnp.maximum(z, 0.0) + jnp.log(1.0 + jnp.exp(-jnp.abs(z)))
            log_1mb = jnp.where(causal, -softplus, 0.0)
            hi, lo = _split_bf16(log_1mb, 2)
            row.append((causal, z - softplus, jnp.sum(log_1mb, axis=-1, keepdims=True),
                        jnp.concatenate([hi, lo], axis=1)))
        mids.append(row)
    between = [[_dot(m[3], uu) for m in row] for row in mids]
    weights, r_out = [], []
    for (_, _, (r, _), _), row, btw in zip(jobs, mids, between):
        ws = []
        for (causal, log_beta, total, _), b in zip(row, btw):
            ws.append(jnp.where(causal, jnp.exp(log_beta + b + r), 0.0).astype(BF16))
            r = r + total
        weights.append(ws)
        r_out.append(r)
    out = []
    for (_, _, (_, acc), tiles), ws, r in zip(jobs, weights, r_out):
        for w, (_, vt, _) in zip(ws, tiles):
            acc = acc + _dot_nt(w, vt)
        out.append((r, acc))
    return out


def _gate_rows(gate, g, j, nq):
    cols = [gate[:, (g * HPG + h) * 3 + j:(g * HPG + h) * 3 + j + 1] for h in range(HPG)]
    return jnp.concatenate(cols, axis=0)


def _lane_lt(shape, n):
    return lax.broadcasted_iota(jnp.int32, shape, 1) < n


def _sb_prompt_kernel(q_ref, kv_ref, uu_ref, o_ref, r_ref, acc_ref):
    qt = pl.program_id(1)
    nq = Q_TILE
    n_pair = H_SB // 2
    qpos = qt * nq + lax.broadcasted_iota(jnp.int32, (nq, 1), 0)
    qpos_rows = _rep_rows(qpos, 2)
    kcol = lax.broadcasted_iota(jnp.int32, (1, KEY_TILE), 1)
    r_ref[...] = jnp.zeros(r_ref.shape, F32)
    acc_ref[...] = jnp.zeros(acc_ref.shape, F32)

    def cond(c):
        return jnp.logical_and(c[0] <= qt, c[1] > SB_EXIT)

    def body(c):
        kt = qt - c[0]
        kpos = kt * KEY_TILE + kcol
        jobs = []
        for p in range(n_pair):
            qs = jnp.concatenate([q_ref[:, (2 * p) * LANES:(2 * p + 1) * LANES],
                                  q_ref[:, (2 * p + 1) * LANES:(2 * p + 2) * LANES]], axis=0)
            jobs.append((qs, qpos_rows, (r_ref[p], acc_ref[p]),
                         [(kv_ref[kt, p * LANES:(p + 1) * LANES, :],
                           kv_ref[kt, SB_W + p * LANES:SB_W + (p + 1) * LANES, :], kpos)]))
        r_max = None
        for p, (r, acc) in enumerate(_sb_tiles(jobs, uu_ref[...])):
            r_ref[p] = r
            acc_ref[p] = acc
            r_max = r if r_max is None else jnp.maximum(r_max, r)
        return c[0] + 1, jnp.max(r_max)

    lax.while_loop(cond, body, (jnp.int32(0), jnp.float32(0.0)))
    low = _lane_lt((nq, LANES), HEAD_DIM)
    for p in range(n_pair):
        acc = acc_ref[p]
        o_ref[:, p * LANES:(p + 1) * LANES] = jnp.where(low, acc[:nq], acc[nq:]).astype(BF16)


def _sb_prompt(q_sb_pad, sb_tiles, uu, batch, seq):
    nqt = seq // Q_TILE
    return pl.pallas_call(
        _sb_prompt_kernel,
        grid=(batch, nqt),
        in_specs=[pl.BlockSpec((Q_TILE, H_SB * LANES), lambda b, t: (b * nqt + t, 0)),
                  pl.BlockSpec((None,) + sb_tiles.shape[1:], lambda b, t: (b, 0, 0, 0)),
                  pl.BlockSpec(uu.shape, lambda b, t: (0, 0))],
        out_specs=pl.BlockSpec((Q_TILE, SB_W), lambda b, t: (b * nqt + t, 0)),
        out_shape=jax.ShapeDtypeStruct((batch * seq, SB_W), BF16),
        scratch_shapes=[pltpu.VMEM((H_SB // 2, 2 * Q_TILE, 1), F32),
                        pltpu.VMEM((H_SB // 2, 2 * Q_TILE, LANES), F32)],
        compiler_params=_cparams(2),
        name="sb_prompt",
    )(q_sb_pad, sb_tiles, uu)


def _nsa_prompt_kernel(q_ref, gate_ref, ht_ref, slc_ref, win_ref, mblk_ref, e_ref, o_ref,
                       sel_ref, qs_ref, part_ref, m_ref, l_ref, acc_ref, *, n_sb):
    qt = pl.program_id(1)
    nq = Q_TILE
    rows = HPG * nq
    qpos = qt * nq + lax.broadcasted_iota(jnp.int32, (nq, 1), 0)
    qpos_rows = _rep_rows(qpos, HPG)
    kcol = lax.broadcasted_iota(jnp.int32, (1, KEY_TILE), 1)
    gate = gate_ref[...]
    cmpk, cmpv = _cmp_tables(ht_ref[...])
    sel = []
    for g in range(G_NSA):
        qs = jnp.concatenate([q_ref[:, (g * HPG + h) * LANES:(g * HPG + h + 1) * LANES]
                              for h in range(HPG)], axis=0)
        qs_ref[g] = qs
        p, o_c = _cmp_branch(qs, cmpk, cmpv, qpos_rows, _slope_rows(g, nq))
        part_ref[g] = _gate_rows(gate, g, 0, nq) * o_c
        imp = p[0:nq] + p[nq:2 * nq] + p[2 * nq:3 * nq] + p[3 * nq:4 * nq]
        sel.append(_select_blocks(imp, mblk_ref[...], qpos, n_sb, blocks_on_sublanes=True))
        sel_ref[g] = sel[g].astype(BF16)
    any_sel = jnp.max(jnp.maximum(sel[0], sel[1]), axis=0, keepdims=True)
    tile_of_block = jnp.right_shift(lax.broadcasted_iota(jnp.int32, any_sel.shape, 1),
                                    (KEY_TILE // SEL_BLOCK).bit_length() - 1)
    m_ref[...] = jnp.full(m_ref.shape, NEG, F32)
    l_ref[...] = jnp.zeros(l_ref.shape, F32)
    acc_ref[...] = jnp.zeros(acc_ref.shape, F32)

    def tile_update(branch, tile_ref, kt):
        jobs = []
        for g in range(G_NSA):
            s = branch * G_NSA + g
            jobs.append((qs_ref[g], tile_ref[kt, :KV_W, :], tile_ref[kt, KV_W:, :], _slope_rows(g, nq),
                         qpos_rows, kt * KEY_TILE + kcol, (m_ref[s], l_ref[s], acc_ref[s]),
                         sel_ref[g] if branch == 0 else None, e_ref[kt] if branch == 0 else None))
        for g, (m, l, acc) in enumerate(_softmax_tiles(jobs, window=branch == 1)):
            s = branch * G_NSA + g
            m_ref[s] = m
            l_ref[s] = l
            acc_ref[s] = acc

    def slc_body(kt, c):
        wanted = jnp.max(jnp.where(tile_of_block == kt, any_sel, 0.0)) > 0.5

        @pl.when(wanted)
        def _():
            tile_update(0, slc_ref, kt)

        return c

    lax.fori_loop(0, qt + 1, slc_body, 0)

    def win_body(kt, c):
        tile_update(1, win_ref, kt)
        return c

    lax.fori_loop(jnp.maximum(qt - WINDOW // KEY_TILE, 0), qt + 1, win_body, 0)
    outs = [part_ref[g]
            + _gate_rows(gate, g, 1, nq) * _softmax_finish((None, l_ref[g], acc_ref[g]))
            + _gate_rows(gate, g, 2, nq) * _softmax_finish((None, l_ref[G_NSA + g], acc_ref[G_NSA + g]))
            for g in range(G_NSA)]
    low = _lane_lt((nq, LANES), HEAD_DIM)
    for h in range(HPG):
        o = jnp.where(low, outs[0][h * nq:(h + 1) * nq], outs[1][h * nq:(h + 1) * nq])
        o_ref[:, h * LANES:(h + 1) * LANES] = o.astype(BF16)


def _nsa_prompt(q_n_pad, gate, ht, slc_tiles, win_tiles, m_blk, e_tiles, batch, seq):
    nqt = seq // Q_TILE
    n_chunk = seq // CMP_STRIDE
    seq_blk = lambda a: pl.BlockSpec((None,) + a.shape[1:], lambda b, t: (b,) + (0,) * (a.ndim - 1))
    return pl.pallas_call(
        functools.partial(_nsa_prompt_kernel, n_sb=seq // SEL_BLOCK),
        grid=(batch, nqt),
        in_specs=[pl.BlockSpec((Q_TILE, H_NSA * LANES), lambda b, t: (b * nqt + t, 0)),
                  pl.BlockSpec((Q_TILE, LANES), lambda b, t: (b * nqt + t, 0)),
                  pl.BlockSpec((None, n_chunk, 4 * KV_W), lambda b, t: (b, 0, 0)),
                  seq_blk(slc_tiles), seq_blk(win_tiles),
                  pl.BlockSpec(m_blk.shape, lambda b, t: (0, 0)),
                  pl.BlockSpec(e_tiles.shape, lambda b, t: (0, 0, 0))],
        out_specs=pl.BlockSpec((Q_TILE, NSA_QW), lambda b, t: (b * nqt + t, 0)),
        out_shape=jax.ShapeDtypeStruct((batch * seq, NSA_QW), BF16),
        scratch_shapes=[pltpu.VMEM((G_NSA, Q_TILE, m_blk.shape[1]), BF16),
                        pltpu.VMEM((G_NSA, HPG * Q_TILE, LANES), BF16),
                        pltpu.VMEM((G_NSA, HPG * Q_TILE, LANES), F32),
                        pltpu.VMEM((2 * G_NSA, HPG * Q_TILE, 1), F32),
                        pltpu.VMEM((2 * G_NSA, HPG * Q_TILE, 1), F32),
                        pltpu.VMEM((2 * G_NSA, HPG * Q_TILE, LANES), F32)],
        compiler_params=_cparams(2),
        name="nsa_prompt",
    )(q_n_pad, gate, ht, slc_tiles, win_tiles, m_blk, e_tiles)


def _new_key_tiles(x):
    padded = jnp.concatenate([x, jnp.zeros((KEY_TILE - Q_PAD, x.shape[1]), F32)], axis=0)
    cols = [padded[:, c * LANES:(c + 1) * LANES].T for c in range(x.shape[1] // LANES)]
    return jnp.concatenate(cols, axis=0).astype(BF16)


def _nsa_sample_local_kernel(q_ref, gate_ref, ht_ref, win_ref, neww_ref, mblk_ref,
                             part_ref, sel_ref, *, past, n_sb):
    nq = Q_PAD
    rows = HPG * nq
    qpos = past + lax.broadcasted_iota(jnp.int32, (nq, 1), 0)
    qpos_rows = _rep_rows(qpos, HPG)
    kcol = lax.broadcasted_iota(jnp.int32, (1, KEY_TILE), 1)
    gate = gate_ref[...]
    cmpk, cmpv = _cmp_tables(ht_ref[...])
    n_w = win_ref.shape[1]
    qs = [q_ref[g * rows:(g + 1) * rows, :] for g in range(G_NSA)]
    slopes = [_slope_rows(g, nq) for g in range(G_NSA)]
    o_c = []
    for g in range(G_NSA):
        p, oc = _cmp_branch(qs[g], cmpk, cmpv, qpos_rows, slopes[g])
        o_c.append(oc)
        imp = p[0:nq] + p[nq:2 * nq] + p[2 * nq:3 * nq] + p[3 * nq:4 * nq]
        sel_ref[g * nq:(g + 1) * nq, :] = _select_blocks(imp, mblk_ref[...], qpos, n_sb,
                                                         blocks_on_sublanes=False)
    tiles = [(win_ref[:, t * KEY_TILE:(t + 1) * KEY_TILE].astype(BF16), (past - n_w + t * KEY_TILE) + kcol)
             for t in range(n_w // KEY_TILE)]
    tiles.append((_new_key_tiles(neww_ref[...]), past + kcol))
    carry = [_softmax_init(rows) for _ in range(G_NSA)]
    for tile, kpos in tiles:
        carry = _softmax_tiles([(qs[g], tile[:KV_W], tile[KV_W:], slopes[g], qpos_rows, kpos, carry[g],
                                 None, None) for g in range(G_NSA)], window=True)
    for g in range(G_NSA):
        part_ref[g * rows:(g + 1) * rows, :] = (_gate_rows(gate, g, 0, nq) * o_c[g]
                                                + _gate_rows(gate, g, 2, nq) * _softmax_finish(carry[g]))


def _nsa_sample_local(q_n_rows, gate8, ht, win_state_t, new_win8, m_blk, past, n_sb):
    n_seq = q_n_rows.shape[0]
    blk = lambda a: pl.BlockSpec((None,) + a.shape[1:], lambda b: (b,) + (0,) * (a.ndim - 1))
    nbp = m_blk.shape[1]
    return pl.pallas_call(
        functools.partial(_nsa_sample_local_kernel, past=past, n_sb=n_sb),
        grid=(n_seq,),
        in_specs=[blk(q_n_rows), blk(gate8), blk(ht), blk(win_state_t), blk(new_win8),
                  pl.BlockSpec(m_blk.shape, lambda b: (0, 0))],
        out_specs=[pl.BlockSpec((None, H_NSA * Q_PAD, LANES), lambda b: (b, 0, 0)),
                   pl.BlockSpec((None, G_NSA * Q_PAD, nbp), lambda b: (b, 0, 0))],
        out_shape=[jax.ShapeDtypeStruct((n_seq, H_NSA * Q_PAD, LANES), F32),
                   jax.ShapeDtypeStruct((n_seq, G_NSA * Q_PAD, nbp), F32)],
        compiler_params=_cparams(1),
        name="nsa_sample_local",
    )(q_n_rows, gate8, ht, win_state_t, new_win8, m_blk)


def _stream_sample_kernel(*refs, per_step, n_pages, past):
    pt_ref, need_ref = refs[0], refs[1]
    del pt_ref
    refs = refs[2:]
    sb_pages = refs[:per_step]
    nsa_pages = refs[per_step:2 * per_step]
    (qsb_ref, qn_ref, newsb_ref, newslc_ref, sel_ref, part_ref, gate_ref, e_ref, uu_ref,
     osb_ref, onsa_ref, sbr_ref, sbacc_ref, m_ref, l_ref, acc_ref, live_ref) = refs[2 * per_step:]
    b = pl.program_id(0)
    j = pl.program_id(1)
    nq = Q_PAD
    rows = HPG * nq
    n_pair = H_SB // 2
    qpos = past + lax.broadcasted_iota(jnp.int32, (nq, 1), 0)
    qpos2 = _rep_rows(qpos, 2)
    qpos4 = _rep_rows(qpos, HPG)
    kcol = lax.broadcasted_iota(jnp.int32, (1, KEY_TILE), 1)

    def sb_visit(tiles):
        jobs = [(qsb_ref[pair * 2 * nq:(pair + 1) * 2 * nq, :], qpos2, (sbr_ref[pair], sbacc_ref[pair]),
                 [(tile[pair * LANES:(pair + 1) * LANES], tile[SB_W + pair * LANES:SB_W + (pair + 1) * LANES],
                   kpos) for tile, kpos in tiles]) for pair in range(n_pair)]
        r_max = None
        for pair, (r, acc) in enumerate(_sb_tiles(jobs, uu_ref[...])):
            sbr_ref[pair] = r
            sbacc_ref[pair] = acc
            r_max = r if r_max is None else jnp.maximum(r_max, r)
        live_ref[0] = (jnp.max(r_max) > SB_EXIT).astype(jnp.int32)

    def slc_visit(tile, e_tile, kpos):
        jobs = [(qn_ref[g * rows:(g + 1) * rows, :], tile[:KV_W], tile[KV_W:], _slope_rows(g, nq), qpos4, kpos,
                 (m_ref[g], l_ref[g], acc_ref[g]), sel_ref[g * nq:(g + 1) * nq, :].astype(BF16), e_tile)
                for g in range(G_NSA)]
        for g, (m, l, acc) in enumerate(_softmax_tiles(jobs, window=False)):
            m_ref[g] = m
            l_ref[g] = l
            acc_ref[g] = acc

    @pl.when(j == 0)
    def _():
        sbr_ref[...] = jnp.zeros(sbr_ref.shape, F32)
        sbacc_ref[...] = jnp.zeros(sbacc_ref.shape, F32)
        m_ref[...] = jnp.full(m_ref.shape, NEG, F32)
        l_ref[...] = jnp.zeros(l_ref.shape, F32)
        acc_ref[...] = jnp.zeros(acc_ref.shape, F32)
        sb_visit([(_new_key_tiles(newsb_ref[...]), past + kcol)])
        slc_visit(_new_key_tiles(newslc_ref[...]), e_ref[n_pages], past + kcol)

    group = 2
    for k0 in range(0, per_step, group):
        @pl.when(live_ref[0] != 0)
        def _(k0=k0):
            tiles = []
            for k in range(k0, min(k0 + group, per_step)):
                pg = n_pages - 1 - (j * per_step + k)
                tiles.append((sb_pages[k][...].astype(BF16), pg * KEY_TILE + kcol))
            sb_visit(tiles)

    for k in range(per_step):
        pg = n_pages - 1 - (j * per_step + k)

        @pl.when(need_ref[b * n_pages + pg] != 0)
        def _(k=k, pg=pg):
            slc_visit(nsa_pages[k][...].astype(BF16), e_ref[pg], pg * KEY_TILE + kcol)

    @pl.when(j == pl.num_programs(1) - 1)
    def _():
        low = _lane_lt((nq, LANES), HEAD_DIM)
        for pair in range(n_pair):
            acc = sbacc_ref[pair]
            osb_ref[:, pair * LANES:(pair + 1) * LANES] = jnp.where(low, acc[:nq], acc[nq:]).astype(BF16)
        gate = gate_ref[...]
        outs = []
        for g in range(G_NSA):
            o_s = _softmax_finish((m_ref[g], l_ref[g], acc_ref[g]))
            outs.append(part_ref[g * rows:(g + 1) * rows, :] + _gate_rows(gate, g, 1, nq) * o_s)
        for h in range(HPG):
            o = jnp.where(low, outs[0][h * nq:(h + 1) * nq], outs[1][h * nq:(h + 1) * nq])
            onsa_ref[:, h * LANES:(h + 1) * LANES] = o.astype(BF16)


def _stream_sample(pages_flat, need_flat, cache_sb_t, cache_nsa_t, q_sb_rows, q_n_rows, new_sb8, new_slc8,
                   sel, part, gate8, e_tiles, uu, n_pages, past):
    n_seq = q_sb_rows.shape[0]
    per_step = min(8, n_pages)
    n_steps = n_pages // per_step

    def page_idx(b, j, k, pt):
        return pt[b * n_pages + (n_pages - 1 - (j * per_step + k))]

    def sb_spec(k):
        return pl.BlockSpec((None, 2 * SB_W, KEY_TILE),
                            lambda b, j, pt, need: (page_idx(b, j, k, pt), 0, 0))

    def nsa_spec(k):
        return pl.BlockSpec((None, 2 * KV_W, KEY_TILE),
                            lambda b, j, pt, need: (page_idx(b, j, k, pt), 1, 0))

    seq_blk = lambda a: pl.BlockSpec((None,) + a.shape[1:],
                                     lambda b, j, pt, need: (b,) + (0,) * (a.ndim - 1))
    const = lambda a: pl.BlockSpec(a.shape, lambda b, j, pt, need: (0,) * a.ndim)
    grid_spec = pltpu.PrefetchScalarGridSpec(
        num_scalar_prefetch=2,
        grid=(n_seq, n_steps),
        in_specs=[sb_spec(k) for k in range(per_step)] + [nsa_spec(k) for k in range(per_step)]
        + [seq_blk(q_sb_rows), seq_blk(q_n_rows), seq_blk(new_sb8), seq_blk(new_slc8),
           seq_blk(sel), seq_blk(part), seq_blk(gate8), const(e_tiles), const(uu)],
        out_specs=[pl.BlockSpec((None, Q_PAD, SB_W), lambda b, j, pt, need: (b, 0, 0)),
                   pl.BlockSpec((None, Q_PAD, NSA_QW), lambda b, j, pt, need: (b, 0, 0))],
        scratch_shapes=[pltpu.VMEM((H_SB // 2, 2 * Q_PAD, 1), F32),
                        pltpu.VMEM((H_SB // 2, 2 * Q_PAD, LANES), F32),
                        pltpu.VMEM((G_NSA, HPG * Q_PAD, 1), F32),
                        pltpu.VMEM((G_NSA, HPG * Q_PAD, 1), F32),
                        pltpu.VMEM((G_NSA, HPG * Q_PAD, LANES), F32),
                        pltpu.SMEM((1,), jnp.int32)],
    )
    return pl.pallas_call(
        functools.partial(_stream_sample_kernel, per_step=per_step, n_pages=n_pages, past=past),
        grid_spec=grid_spec,
        out_shape=[jax.ShapeDtypeStruct((n_seq, Q_PAD, SB_W), BF16),
                   jax.ShapeDtypeStruct((n_seq, Q_PAD, NSA_QW), BF16)],
        compiler_params=_cparams(2),
        name="stream_sample",
    )(pages_flat, need_flat, *([cache_sb_t] * per_step), *([cache_nsa_t] * per_step), q_sb_rows, q_n_rows,
      new_sb8, new_slc8, sel, part, gate8, e_tiles, uu)


def _block_sum_matrix(n_chunk_pad, n_c, n_sb, nbp):
    per = SEL_BLOCK // CMP_STRIDE
    c = jnp.arange(n_chunk_pad)[:, None]
    b = jnp.arange(nbp)[None, :]
    own = (c // per == b)
    nxt = jnp.logical_and((c + 1) // per == b, c + 1 < n_sb * per)
    m = (own.astype(F32) + nxt.astype(F32)) * (c < n_c) * (b < n_sb)
    return m.astype(BF16)


def _expand_tiles(n_tiles, nbp):
    t = jnp.arange(n_tiles)[:, None, None]
    b = jnp.arange(nbp)[None, :, None]
    k = jnp.arange(KEY_TILE)[None, None, :]
    return ((t * KEY_TILE + k) // SEL_BLOCK == b).astype(BF16)


def _suffix_matrix():
    j = jnp.arange(KEY_TILE)[:, None]
    s = jnp.arange(KEY_TILE)[None, :]
    u = (j > s).astype(BF16)
    return jnp.concatenate([u, u], axis=0)


def _rows_per_seq(x, n_seq, n_new, slots):
    x = x.reshape(n_seq, n_new, slots, LANES)
    x = jnp.pad(x, ((0, 0), (0, Q_PAD - n_new), (0, 0), (0, 0)))
    return x.transpose(0, 2, 1, 3).reshape(n_seq, slots * Q_PAD, LANES)


def _pad_new(x, n_seq, n_new):
    x = x.reshape(n_seq, n_new, x.shape[-1])
    return jnp.pad(x, ((0, 0), (0, Q_PAD - n_new), (0, 0)))


def _token_minor(x, lead):
    n = x.ndim
    perm = tuple(range(lead)) + tuple(range(lead + 1, n)) + (lead,)
    x = x.transpose(perm)
    return x.reshape(x.shape[:lead] + (-1, x.shape[-1]))


def _token_major(x_t, lead, feat_shape):
    x = x_t.reshape(x_t.shape[:lead] + tuple(feat_shape) + (x_t.shape[-1],))
    n = x.ndim
    perm = tuple(range(lead)) + (n - 1,) + tuple(range(lead, n - 1))
    return x.transpose(perm)


def kernel(x_prompt, x_sample, cache_sb_kv, cache_nsa_kv, state_win_kv, page_table, c_prompt, c_sample,
           w_ada, b_ada, w_ff1_up, w_ff1_down, w_in, w_cmp_k, w_cmp_v, w_o, w_ff2_up, w_ff2_down,
           ln_g, ln_b):
    batch, seq, d = x_prompt.shape
    n_seq, n_new, _ = x_sample.shape
    depth, n_pool, page = cache_sb_kv.shape[:3]
    n_pages = page_table.shape[1]
    past = n_pages * page
    n_w = state_win_kv.shape[2]
    alpha = (2.0 * depth) ** 0.25
    assert page == KEY_TILE and seq % TOKEN_TILE == 0 and n_new <= Q_PAD and n_w % KEY_TILE == 0
    assert n_new < CMP_STRIDE and past % SEL_BLOCK == 0

    uu = _suffix_matrix()
    n_chunk_p = seq // CMP_STRIDE
    n_sb_p = seq // SEL_BLOCK
    m_blk_p = _block_sum_matrix(n_chunk_p, n_chunk_p - 1, n_sb_p, LANES * pl.cdiv(n_sb_p, LANES))
    e_p = _expand_tiles(seq // KEY_TILE, m_blk_p.shape[1])
    t_all = past + n_new
    n_sb_s = pl.cdiv(t_all, SEL_BLOCK)
    n_chunk_s = past // CMP_STRIDE
    m_blk_s = _block_sum_matrix(n_chunk_s, t_all // CMP_STRIDE - 1, n_sb_s,
                                LANES * pl.cdiv(n_sb_s, LANES))
    e_s = _expand_tiles(n_pages + 1, m_blk_s.shape[1])
    blocks_per_page = KEY_TILE // SEL_BLOCK

    xp = x_prompt.reshape(batch * seq, d)
    xs = x_sample.reshape(n_seq * n_new, d)
    c_all = jnp.concatenate([c_prompt, c_sample], axis=0)
    cache_sb_t = _token_minor(cache_sb_kv, 2).reshape(depth * n_pool, 2 * SB_W, page)
    cache_nsa_t = _token_minor(cache_nsa_kv, 2).reshape(depth * n_pool, 4 * KV_W, page)
    win_state_t = _token_minor(state_win_kv, 2)
    tiles_per_seq = seq // TOKEN_TILE
    states = [[] for _ in range(6)]

    for layer in range(depth):
        mod = _ada(c_all, w_ada[layer], b_ada[layer]).reshape(batch + n_seq, N_MOD, d)
        mod_p = mod[:batch].transpose(1, 0, 2)[:, :, None, :]
        mod_s = jnp.repeat(mod[batch:], n_new, axis=0).transpose(1, 0, 2)[:, None, :, :]
        w_up1, w_dn1 = w_ff1_up[layer].astype(BF16), w_ff1_down[layer].astype(BF16)
        w_up2, w_dn2 = w_ff2_up[layer].astype(BF16), w_ff2_down[layer].astype(BF16)
        w_proj = _proj_weights(w_in[layer])
        w_kv_t = w_proj[:, _P_KVSB:].T
        w_cmp = _compress_weights(w_cmp_k[layer], w_cmp_v[layer])
        w_osb = w_o[layer, :SB_W].astype(BF16)
        w_onsa = (w_o[layer, SB_W:].reshape(G_NSA, HPG, HEAD_DIM, d).transpose(1, 0, 2, 3)
                  .reshape(NSA_QW, d).astype(BF16))
        lng, lnb = ln_g[layer], ln_b[layer]
        pages_flat = (page_table + layer * n_pool).reshape(-1).astype(jnp.int32)

        kw_p = dict(per_token=False, tiles_per_seq=tiles_per_seq)
        x1 = _ffn(xp, mod_p, w_up1, w_dn1, lng[0:1], lnb[0:1], mod_base=0, alpha=alpha, **kw_p)
        (q_sb, q_n, gate, cmp_rows, kv_sb_t, kv_nsa_t, kv_win_t, sb_tiles, slc_tiles, win_tiles) = _proj_prompt(
            x1, mod_p, w_proj, w_kv_t, batch, seq, mod_base=3)
        ht = _compress_prompt(cmp_rows, w_cmp, batch, seq)
        o_sb = _sb_prompt(q_sb, sb_tiles, uu, batch, seq)
        o_nsa = _nsa_prompt(q_n, gate, ht, slc_tiles, win_tiles, m_blk_p, e_p, batch, seq)
        xp = _ffn(x1, mod_p, w_up2, w_dn2, lng[1:3], lnb[1:3], mod_base=6, alpha=alpha,
                  oproj=(o_sb, o_nsa, w_osb, w_onsa), **kw_p)
        n_keep = min(WINDOW, seq)
        states[0].append(_token_major(kv_sb_t, 1, (2, H_SB, HEAD_DIM)))
        states[1].append(_token_major(kv_nsa_t, 1, (4, G_NSA, HEAD_DIM)))
        states[2].append(_token_major(kv_win_t[:, :, seq - n_keep:], 1, (2, G_NSA, HEAD_DIM)))

        kw_s = dict(per_token=True, tiles_per_seq=1)
        s1 = _ffn(xs, mod_s, w_up1, w_dn1, lng[0:1], lnb[0:1], mod_base=0, alpha=alpha, **kw_s)
        q_sb, q_n, gate, kv_sb, kv_nsa, kv_win = _proj_sample(s1, mod_s, w_proj, mod_base=3)
        ht = _compress_pages(cache_nsa_t, pages_flat, w_cmp, n_seq, n_pages)
        q_n_rows = _rows_per_seq(q_n, n_seq, n_new, H_NSA)
        q_sb_rows = _rows_per_seq(q_sb, n_seq, n_new, H_SB)
        gate8 = _pad_new(gate, n_seq, n_new)
        part, sel = _nsa_sample_local(q_n_rows, gate8, ht, win_state_t[layer],
                                      _pad_new(kv_win, n_seq, n_new), m_blk_s, past, n_sb_s)
        page_sel = sel[:, :, :n_pages * blocks_per_page].reshape(n_seq, -1, n_pages, blocks_per_page)
        need_flat = (jnp.max(page_sel, axis=(1, 3)) > 0.5).astype(jnp.int32).reshape(-1)
        o_sb, o_nsa = _stream_sample(pages_flat, need_flat, cache_sb_t, cache_nsa_t, q_sb_rows, q_n_rows,
                                     _pad_new(kv_sb, n_seq, n_new),
                                     _pad_new(kv_nsa[:, 2 * KV_W:], n_seq, n_new),
                                     sel, part, gate8, e_s, uu, n_pages, past)
        o_sb = o_sb[:, :n_new].reshape(n_seq * n_new, SB_W)
        o_nsa = o_nsa[:, :n_new].reshape(n_seq * n_new, NSA_QW)
        xs = _ffn(s1, mod_s, w_up2, w_dn2, lng[1:3], lnb[1:3], mod_base=6, alpha=alpha,
                  oproj=(o_sb, o_nsa, w_osb, w_onsa), **kw_s)
        new_win_t = jnp.concatenate([win_state_t[layer][:, :, n_new:],
                                     kv_win.reshape(n_seq, n_new, 2 * KV_W).transpose(0, 2, 1)], axis=2)
        states[3].append(kv_sb.reshape(n_seq, n_new, 2, H_SB, HEAD_DIM))
        states[4].append(kv_nsa.reshape(n_seq, n_new, 4, G_NSA, HEAD_DIM))
        states[5].append(_token_major(new_win_t, 1, (2, G_NSA, HEAD_DIM)))

    return (xp.reshape(batch, seq, d), xs.reshape(n_seq, n_new, d)) + tuple(jnp.stack(s) for s in states)
```

```python
import functools

import jax
import jax.numpy as jnp
from jax import lax
from jax.experimental import pallas as pl
from jax.experimental.pallas import tpu as pltpu

F32 = jnp.float32
BF16 = jnp.bfloat16

HEAD_DIM = 64
H_SB = 8
H_NSA = 8
G_NSA = 2
HPG = H_NSA // G_NSA
CMP_BLOCK = 32
CMP_STRIDE = 16
SEL_BLOCK = 64
N_SELECT = 16
WINDOW = 512
N_MOD = 9
LN_EPS = 1e-5
FORCE_BONUS = 1.0e4
NEG = -1.0e30
MASKED = -2.0e30

SB_W = H_SB * HEAD_DIM
NSA_QW = H_NSA * HEAD_DIM
KV_W = G_NSA * HEAD_DIM
OFF_NSA_Q = 3 * SB_W
OFF_NSA_KV = OFF_NSA_Q + NSA_QW
OFF_WIN = OFF_NSA_KV + 4 * KV_W
OFF_GATE = OFF_WIN + 2 * KV_W
N_IN = OFF_GATE + 3 * H_NSA

LANES = 128
KEY_TILE = 128
VMEM_LIMIT = 56 * 1024 * 1024

TOKEN_TILE = 512
Q_TILE = 128
FF_CHUNK = 256
Q_PAD = 8
SB_EXIT = -88.0


def _cparams(n_axes):
    return pltpu.CompilerParams(dimension_semantics=("arbitrary",) * n_axes,
                                vmem_limit_bytes=VMEM_LIMIT)


def _dot(a, b):
    return jnp.dot(a, b, preferred_element_type=F32)


def _dot_nt(a, b):
    return lax.dot_general(a, b, (((1,), (1,)), ((), ())), preferred_element_type=F32)


def _split_bf16(x, parts):
    out = []
    r = x
    for _ in range(parts):
        h = r.astype(BF16)
        out.append(h)
        r = r - h.astype(F32)
    return out


def _sigmoid(x):
    return 1.0 / (1.0 + jnp.exp(-x))


def _layer_norm(y, g, b):
    mu = jnp.mean(y, axis=-1, keepdims=True)
    d = y - mu
    var = jnp.mean(d * d, axis=-1, keepdims=True)
    return d * lax.rsqrt(var + LN_EPS) * g + b


def _ada_kernel(c_ref, w_ref, b_ref, o_ref):
    c = c_ref[...]
    a = c * _sigmoid(c)
    a_hi, a_lo = _split_bf16(a, 2)
    w_hi, w_lo = _split_bf16(w_ref[...], 2)
    o_ref[...] = _dot(a_hi, w_hi) + (_dot(a_hi, w_lo) + _dot(a_lo, w_hi)) + b_ref[...]


def _ada(c, w, b):
    n, d = c.shape
    n_out = w.shape[1]
    bn = 1024
    return pl.pallas_call(
        _ada_kernel,
        grid=(n_out // bn,),
        in_specs=[pl.BlockSpec((n, d), lambda j: (0, 0)),
                  pl.BlockSpec((d, bn), lambda j: (0, j)),
                  pl.BlockSpec((1, bn), lambda j: (0, j))],
        out_specs=pl.BlockSpec((n, bn), lambda j: (0, j)),
        out_shape=jax.ShapeDtypeStruct((n, n_out), F32),
        compiler_params=_cparams(1),
        name="ada",
    )(c, w, b.reshape(1, n_out))


def _mod_spec(per_token, tm, d, tiles_per_seq):
    if per_token:
        return pl.BlockSpec((N_MOD, None, tm, d), lambda i: (0, 0, i, 0))
    return pl.BlockSpec((N_MOD, None, 1, d), lambda i: (0, i // tiles_per_seq, 0, 0))


def _ffn_kernel(*refs, mod_base, d_ff, alpha, fuse_oproj):
    if fuse_oproj:
        (x_ref, osb_ref, onsa_ref, mod_ref, wosb_ref, wonsa_ref, wup_ref, wdn_ref,
         lng_ref, lnb_ref, o_ref, acc_ref) = refs
        mixed = _dot(osb_ref[...], wosb_ref[...]) + _dot(onsa_ref[...], wonsa_ref[...])
        x = _layer_norm(alpha * x_ref[...] + mod_ref[mod_base - 1] * mixed,
                        lng_ref[0:1, :], lnb_ref[0:1, :])
        ln_row = 1
    else:
        x_ref, mod_ref, wup_ref, wdn_ref, lng_ref, lnb_ref, o_ref, acc_ref = refs
        x = x_ref[...]
        ln_row = 0
    u = (x * (1.0 + mod_ref[mod_base + 1]) + mod_ref[mod_base]).astype(BF16)
    for c in range(d_ff // FF_CHUNK):
        lo = c * FF_CHUNK
        g = _dot(u, wup_ref[:, lo:lo + FF_CHUNK])
        v = _dot(u, wup_ref[:, d_ff + lo:d_ff + lo + FF_CHUNK])
        act = (g * _sigmoid(g) * v).astype(BF16)
        part = _dot(act, wdn_ref[lo:lo + FF_CHUNK, :])
        if c == 0:
            acc_ref[...] = part
        else:
            acc_ref[...] += part
    y = alpha * x + 0.5 * mod_ref[mod_base + 2] * acc_ref[...]
    o_ref[...] = _layer_norm(y, lng_ref[ln_row:ln_row + 1, :], lnb_ref[ln_row:ln_row + 1, :])


def _ffn(x, mod4, w_up, w_down, ln_g2, ln_b2, *, mod_base, alpha, per_token, tiles_per_seq,
         oproj=None):
    t, d = x.shape
    d_ff = w_down.shape[0]
    tm = min(TOKEN_TILE, t)
    row = lambda i: (i, 0)
    const = lambda i: (0, 0)
    in_specs = [pl.BlockSpec((tm, d), row)]
    args = [x]
    if oproj is not None:
        o_sb, o_nsa, w_osb, w_onsa = oproj
        in_specs += [pl.BlockSpec((tm, SB_W), row), pl.BlockSpec((tm, NSA_QW), row)]
        args += [o_sb, o_nsa]
    in_specs.append(_mod_spec(per_token, tm, d, tiles_per_seq))
    args.append(mod4)
    if oproj is not None:
        in_specs += [pl.BlockSpec((SB_W, d), const), pl.BlockSpec((NSA_QW, d), const)]
        args += [w_osb, w_onsa]
    in_specs += [pl.BlockSpec((d, 2 * d_ff), const), pl.BlockSpec((d_ff, d), const),
                 pl.BlockSpec(ln_g2.shape, const), pl.BlockSpec(ln_b2.shape, const)]
    args += [w_up, w_down, ln_g2, ln_b2]
    return pl.pallas_call(
        functools.partial(_ffn_kernel, mod_base=mod_base, d_ff=d_ff, alpha=alpha,
                          fuse_oproj=oproj is not None),
        grid=(t // tm,),
        in_specs=in_specs,
        out_specs=pl.BlockSpec((tm, d), row),
        out_shape=jax.ShapeDtypeStruct((t, d), F32),
        scratch_shapes=[pltpu.VMEM((tm, d), F32)],
        compiler_params=_cparams(1),
        name="ffn_oproj" if oproj is not None else "ffn",
    )(*args)


_P_QSB = 0
_P_QN = _P_QSB + H_SB * LANES
_P_GATE = _P_QN + H_NSA * LANES
_P_KVSB = _P_GATE + LANES
_P_KVN = _P_KVSB + 2 * SB_W
_P_WIN = _P_KVN + 4 * KV_W
_P_END = _P_WIN + 2 * KV_W
_KV_ROWS = _P_END - _P_KVSB


def _proj_weights(w_in):
    d = w_in.shape[0]
    scale = HEAD_DIM ** -0.5
    q_sb = (w_in[:, :SB_W] * scale).reshape(d, H_SB, HEAD_DIM)
    q_sb_pad = jnp.zeros((d, H_SB, LANES), F32)
    for h in range(H_SB):
        o = (h % 2) * HEAD_DIM
        q_sb_pad = q_sb_pad.at[:, h, o:o + HEAD_DIM].set(q_sb[:, h])
    q_n = (w_in[:, OFF_NSA_Q:OFF_NSA_KV] * scale).reshape(d, H_NSA, HEAD_DIM)
    q_n_pad = jnp.zeros((d, H_NSA, LANES), F32).at[:, :, :HEAD_DIM].set(q_n)
    gate = jnp.pad(w_in[:, OFF_GATE:], ((0, 0), (0, LANES - 3 * H_NSA)))
    w = jnp.concatenate([q_sb_pad.reshape(d, -1), q_n_pad.reshape(d, -1), gate, w_in[:, SB_W:3 * SB_W],
                         w_in[:, OFF_NSA_KV:OFF_WIN], w_in[:, OFF_WIN:OFF_GATE]], axis=1)
    return w.astype(BF16)


def _proj_rows(u, w_ref, qsb_ref, qn_ref, gate_ref):
    qsb_ref[...] = _dot(u, w_ref[:, _P_QSB:_P_QN]).astype(BF16)
    qn_ref[...] = _dot(u, w_ref[:, _P_QN:_P_GATE]).astype(BF16)
    gate_ref[...] = _sigmoid(_dot(u, w_ref[:, _P_GATE:_P_KVSB]))


def _proj_sample_kernel(x_ref, mod_ref, w_ref, qsb_ref, qn_ref, gate_ref, kvsb_ref, kvn_ref, win_ref,
                        *, mod_base):
    u = (x_ref[...] * (1.0 + mod_ref[mod_base + 1]) + mod_ref[mod_base]).astype(BF16)
    _proj_rows(u, w_ref, qsb_ref, qn_ref, gate_ref)
    kvsb_ref[...] = _dot(u, w_ref[:, _P_KVSB:_P_KVN])
    kvn_ref[...] = _dot(u, w_ref[:, _P_KVN:_P_WIN])
    win_ref[...] = _dot(u, w_ref[:, _P_WIN:_P_END])


def _proj_sample(x, mod4, w_proj, *, mod_base):
    t, d = x.shape
    row = lambda i: (i, 0)
    widths = [(H_SB * LANES, BF16), (H_NSA * LANES, BF16), (LANES, F32), (2 * SB_W, F32),
              (4 * KV_W, F32), (2 * KV_W, F32)]
    return pl.pallas_call(
        functools.partial(_proj_sample_kernel, mod_base=mod_base),
        grid=(1,),
        in_specs=[pl.BlockSpec((t, d), row), _mod_spec(True, t, d, 1),
                  pl.BlockSpec(w_proj.shape, lambda i: (0, 0))],
        out_specs=[pl.BlockSpec((t, w), row) for w, _ in widths],
        out_shape=[jax.ShapeDtypeStruct((t, w), dt) for w, dt in widths],
        compiler_params=_cparams(1),
        name="proj_sample",
    )(x, mod4, w_proj)


def _proj_prompt_kernel(x_ref, mod_ref, w_ref, wt_ref, qsb_ref, qn_ref, gate_ref, cmp_ref,
                        sbt_ref, nsat_ref, wint_ref, sbtile_ref, slctile_ref, wintile_ref, *, mod_base):
    u = (x_ref[...] * (1.0 + mod_ref[mod_base + 1]) + mod_ref[mod_base]).astype(BF16)
    _proj_rows(u, w_ref, qsb_ref, qn_ref, gate_ref)
    cmp_ref[...] = _dot(u, w_ref[:, _P_KVN:_P_KVN + 2 * KV_W])
    kvt = _dot_nt(wt_ref[...], u)
    n_sb, n_nsa = 2 * SB_W, 4 * KV_W
    sbt_ref[...] = kvt[:n_sb]
    nsat_ref[...] = kvt[n_sb:n_sb + n_nsa]
    wint_ref[...] = kvt[n_sb + n_nsa:]
    kvb = kvt.astype(BF16)
    for j in range(sbtile_ref.shape[0]):
        cols = slice(j * KEY_TILE, (j + 1) * KEY_TILE)
        sbtile_ref[j] = kvb[:n_sb, cols]
        slctile_ref[j] = kvb[n_sb + 2 * KV_W:n_sb + n_nsa, cols]
        wintile_ref[j] = kvb[n_sb + n_nsa:, cols]


def _proj_prompt(x, mod4, w_proj, w_kv_t, batch, seq, *, mod_base):
    t, d = x.shape
    tm = TOKEN_TILE
    tps = seq // tm
    kt = tm // KEY_TILE
    nkt = seq // KEY_TILE
    row = lambda i: (i, 0)
    tr = lambda i: (i // tps, 0, i % tps)
    tile = lambda i: (i // tps, i % tps, 0, 0)
    rows = [(H_SB * LANES, BF16), (H_NSA * LANES, BF16), (LANES, F32), (2 * KV_W, F32)]
    trans = [2 * SB_W, 4 * KV_W, 2 * KV_W]
    tiles = [2 * SB_W, 2 * KV_W, 2 * KV_W]
    return pl.pallas_call(
        functools.partial(_proj_prompt_kernel, mod_base=mod_base),
        grid=(t // tm,),
        in_specs=[pl.BlockSpec((tm, d), row), _mod_spec(False, tm, d, tps),
                  pl.BlockSpec(w_proj.shape, lambda i: (0, 0)),
                  pl.BlockSpec(w_kv_t.shape, lambda i: (0, 0))],
        out_specs=[pl.BlockSpec((tm, w), row) for w, _ in rows]
        + [pl.BlockSpec((None, r, tm), tr) for r in trans]
        + [pl.BlockSpec((None, kt, r, KEY_TILE), tile) for r in tiles],
        out_shape=[jax.ShapeDtypeStruct((t, w), dt) for w, dt in rows]
        + [jax.ShapeDtypeStruct((batch, r, seq), F32) for r in trans]
        + [jax.ShapeDtypeStruct((batch, nkt, r, KEY_TILE), BF16) for r in tiles],
        compiler_params=_cparams(1),
        name="proj_prompt",
    )(x, mod4, w_proj, w_kv_t)


def _compress_weights(w_cmp_k, w_cmp_v):
    n_half = CMP_BLOCK // CMP_STRIDE
    kinds = []
    for w in (w_cmp_k, w_cmp_v):
        w = w.reshape(n_half, CMP_STRIDE, HEAD_DIM, HEAD_DIM)
        blk = jnp.zeros((n_half, CMP_STRIDE, KV_W, KV_W), F32)
        for g in range(G_NSA):
            lo = g * HEAD_DIM
            blk = blk.at[:, :, lo:lo + HEAD_DIM, lo:lo + HEAD_DIM].set(w)
        kinds.append(jnp.concatenate([blk[0], blk[1]], axis=-1))
    return jnp.stack(kinds, axis=1).astype(BF16)


def _compress_rows(xk_ref, xv_ref, w_ref, o_ref):
    n_chunk = o_ref.shape[0]
    for kind, x_ref in enumerate((xk_ref, xv_ref)):
        acc = jnp.zeros((n_chunk, 2 * KV_W), F32)
        for i in range(CMP_STRIDE):
            xi = x_ref[pl.ds(i, n_chunk, stride=CMP_STRIDE), :]
            acc = acc + _dot(xi.astype(BF16), w_ref[i, kind])
        o_ref[:, kind * KV_W:(kind + 1) * KV_W] = acc[:, :KV_W]
        o_ref[:, (2 + kind) * KV_W:(3 + kind) * KV_W] = acc[:, KV_W:]


def _compress_prompt_kernel(xk_ref, xv_ref, w_ref, o_ref):
    _compress_rows(xk_ref, xv_ref, w_ref, o_ref)


def _compress_pages_kernel(*refs, per_step):
    page_refs = refs[1:1 + per_step]
    w_ref, o_ref, xk_ref, xv_ref = refs[1 + per_step:]
    for k, p_ref in enumerate(page_refs):
        rows = slice(k * KEY_TILE, (k + 1) * KEY_TILE)
        xk_ref[rows, :] = p_ref[:KV_W, :].T
        xv_ref[rows, :] = p_ref[KV_W:, :].T
    _compress_rows(xk_ref, xv_ref, w_ref, o_ref)


def _compress_prompt(kv_nsa, w_cmp, batch, seq):
    n_chunk = seq // CMP_STRIDE
    return pl.pallas_call(
        _compress_prompt_kernel,
        grid=(batch,),
        in_specs=[pl.BlockSpec((seq, KV_W), lambda b: (b, 0)),
                  pl.BlockSpec((seq, KV_W), lambda b: (b, 1)),
                  pl.BlockSpec(w_cmp.shape, lambda b: (0, 0, 0, 0))],
        out_specs=pl.BlockSpec((None, n_chunk, 4 * KV_W), lambda b: (b, 0, 0)),
        out_shape=jax.ShapeDtypeStruct((batch, n_chunk, 4 * KV_W), F32),
        compiler_params=_cparams(1),
        name="compress_prompt",
    )(kv_nsa, kv_nsa, w_cmp)


def _compress_pages(cache_nsa_t, pages_flat, w_cmp, n_seq, n_pages):
    per_step = min(32, n_pages)
    n_steps = n_pages // per_step
    chunks = per_step * KEY_TILE // CMP_STRIDE

    def page_spec(k):
        return pl.BlockSpec((None, 2 * KV_W, KEY_TILE),
                            lambda b, j, pt: (pt[b * n_pages + j * per_step + k], 0, 0))

    grid_spec = pltpu.PrefetchScalarGridSpec(
        num_scalar_prefetch=1,
        grid=(n_seq, n_steps),
        in_specs=[page_spec(k) for k in range(per_step)]
        + [pl.BlockSpec(w_cmp.shape, lambda b, j, pt: (0, 0, 0, 0))],
        out_specs=pl.BlockSpec((None, chunks, 4 * KV_W), lambda b, j, pt: (b, j, 0)),
        scratch_shapes=[pltpu.VMEM((per_step * KEY_TILE, KV_W), F32),
                        pltpu.VMEM((per_step * KEY_TILE, KV_W), F32)],
    )
    return pl.pallas_call(
        functools.partial(_compress_pages_kernel, per_step=per_step),
        grid_spec=grid_spec,
        out_shape=jax.ShapeDtypeStruct((n_seq, n_steps * chunks, 4 * KV_W), F32),
        compiler_params=_cparams(2),
        name="compress_pages",
    )(pages_flat, *([cache_nsa_t] * per_step), w_cmp)


def _rep_rows(x, n):
    return jnp.concatenate([x] * n, axis=0)


def _masked_softmax(s, mask):
    s = jnp.where(mask, s, NEG)
    m = jnp.max(s, axis=-1, keepdims=True)
    e = jnp.where(mask, jnp.exp(s - m), 0.0)
    return e / jnp.maximum(jnp.sum(e, axis=-1, keepdims=True), 1e-30)


def _with_positions(x, pos):
    lane = lax.broadcasted_iota(jnp.int32, x.shape, x.ndim - 1)
    hi = jnp.right_shift(pos, 8).astype(F32)
    lo = jnp.bitwise_and(pos, 255).astype(F32)
    return jnp.where(lane == HEAD_DIM, hi, jnp.where(lane == HEAD_DIM + 1, lo, x))


def _query_rows(q_slots, heads, extra=None):
    rows = []
    for q, h in zip(q_slots, heads):
        slope = 2.0 ** -(h + 1)
        lane = lax.broadcasted_iota(jnp.int32, q.shape, 1)
        q = jnp.where(lane == HEAD_DIM, 256.0 * slope, jnp.where(lane == HEAD_DIM + 1, slope, q.astype(F32)))
        rows.append(q if extra is None else jnp.concatenate([q, extra], axis=1))
    return jnp.concatenate(rows, axis=0).astype(BF16)


def _cmp_tables(ht):
    n_chunk = ht.shape[0]
    cmp = ht[:, :2 * KV_W] + pltpu.roll(ht[:, 2 * KV_W:], n_chunk - 1, 0)
    k = cmp[:, :KV_W]
    low = _lane_lt(k.shape, HEAD_DIM)
    cend = lax.broadcasted_iota(jnp.int32, (n_chunk, 1), 0) * CMP_STRIDE + (CMP_BLOCK - 1)
    keys = [_with_positions(jnp.where(low, k if g == 0 else pltpu.roll(k, HEAD_DIM, 1), 0.0), cend).astype(BF16)
            for g in range(G_NSA)]
    return keys, cmp[:, KV_W:].astype(BF16)


def _cmp_branch(qs, cmpk, cmpv, qpos_rows):
    n_chunk = cmpk.shape[0]
    s = _dot_nt(qs, cmpk)
    cidx = lax.broadcasted_iota(jnp.int32, (1, n_chunk), 1)
    cend = cidx * CMP_STRIDE + (CMP_BLOCK - 1)
    mask = jnp.logical_and(cend <= qpos_rows, cidx < n_chunk - 1)
    p = _masked_softmax(s, mask)
    return p, _dot(p.astype(BF16), cmpv)


def _select_blocks(imp, m_blk, qpos_col, n_sb, *, blocks_on_sublanes):
    nbp = m_blk.shape[1]
    imp_blk = sum(_dot(part, m_blk) for part in _split_bf16(imp, 3))
    bidx = lax.broadcasted_iota(jnp.int32, (1, nbp), 1)
    tb = jnp.right_shift(qpos_col, SEL_BLOCK.bit_length() - 1)
    valid = jnp.logical_and(bidx * SEL_BLOCK <= qpos_col, bidx < n_sb)
    forced = jnp.logical_or(bidx == 0, jnp.logical_or(bidx == tb, bidx == tb - 1))
    score = jnp.where(valid, imp_blk + jnp.where(forced, FORCE_BONUS, 0.0), -jnp.inf)
    k_sel = min(N_SELECT, n_sb)
    if blocks_on_sublanes:
        st = score.T
        ridx = lax.broadcasted_iota(jnp.int32, (nbp, 1), 0)
        cnt = jnp.zeros(st.shape, F32)
        for bp in range(n_sb):
            row = st[bp:bp + 1, :]
            beats = jnp.logical_or(row > st, jnp.logical_and(row == st, ridx > bp))
            cnt = cnt + jnp.where(beats, 1.0, 0.0)
        few = jnp.where(cnt < k_sel, 1.0, 0.0).T
    else:
        cnt = jnp.zeros(score.shape, F32)
        for bp in range(n_sb):
            col = score[:, bp:bp + 1]
            beats = jnp.logical_or(col > score, jnp.logical_and(col == score, bidx > bp))
            cnt = cnt + jnp.where(beats, 1.0, 0.0)
        few = jnp.where(cnt < k_sel, 1.0, 0.0)
    return jnp.where(valid, few, 0.0)


def _softmax_tiles(jobs, mask):
    logits = [_dot(j[0], j[1]) for j in jobs]
    mids = []
    for s, (_, _, _, (m, _)) in zip(logits, jobs):
        if mask is not None:
            s = jnp.where(mask, s, MASKED)
        m_new = jnp.maximum(m, jnp.max(s, axis=-1, keepdims=True))
        mids.append((m_new, jnp.exp(m - m_new), jnp.exp(s - m_new).astype(BF16)))
    return [(m_new, a * j[3][1] + _dot_nt(p, j[2])) for (m_new, a, p), j in zip(mids, jobs)]


def _value_tile(vt, g):
    ones = jnp.ones((HEAD_DIM, vt.shape[1]), BF16)
    return jnp.concatenate([vt[:HEAD_DIM], ones] if g == 0 else [ones, vt[HEAD_DIM:]], axis=0)


def _key_tile(kt, g, pos_rows, mask_rows=None):
    parts = [kt[g * HEAD_DIM:(g + 1) * HEAD_DIM], pos_rows]
    if mask_rows is not None:
        parts.append(mask_rows)
    return jnp.concatenate(parts, axis=0)


def _softmax_init(rows):
    return (jnp.full((rows, LANES), NEG, F32), jnp.zeros((rows, LANES), F32))


def _softmax_finish(acc, g):
    lane = (1 - g) * HEAD_DIM
    return acc / jnp.maximum(acc[:, lane:lane + 1], 1e-30)


def _sb_tiles(jobs, uu):
    logits = [[_dot(qs, kt) for kt, _, _ in tiles] for qs, _, _, tiles in jobs]
    mids = []
    for (_, qpos_rows, _, tiles), zs in zip(jobs, logits):
        row = []
        for z, (_, _, kpos_cols) in zip(zs, tiles):
            causal = kpos_cols < qpos_rows
            softplus = jnp.maximum(z, 0.0) + jnp.log(1.0 + jnp.exp(-jnp.abs(z)))
            log_1mb = jnp.where(causal, -softplus, 0.0)
            hi, lo = _split_bf16(log_1mb, 2)
            row.append((causal, z - softplus, jnp.sum(log_1mb, axis=-1, keepdims=True),
                        jnp.concatenate([hi, lo], axis=1)))
        mids.append(row)
    between = [[_dot(m[3], uu) for m in row] for row in mids]
    weights, r_out = [], []
    for (_, _, (r, _), _), row, btw in zip(jobs, mids, between):
        ws = []
        for (causal, log_beta, total, _), b in zip(row, btw):
            ws.append(jnp.where(causal, jnp.exp(log_beta + b + r), 0.0).astype(BF16))
            r = r + total
        weights.append(ws)
        r_out.append(r)
    out = []
    for (_, _, (_, acc), tiles), ws, r in zip(jobs, weights, r_out):
        for w, (_, vt, _) in zip(ws, tiles):
            acc = acc + _dot_nt(w, vt)
        out.append((r, acc))
    return out


def _gate_rows(gate, g, j, nq):
    cols = [gate[:, (g * HPG + h) * 3 + j:(g * HPG + h) * 3 + j + 1] for h in range(HPG)]
    return jnp.concatenate(cols, axis=0)


def _lane_lt(shape, n):
    return lax.broadcasted_iota(jnp.int32, shape, 1) < n


def _sb_prompt_kernel(q_ref, kv_ref, uu_ref, o_ref, r_ref, acc_ref):
    qt = pl.program_id(1)
    nq = Q_TILE
    n_pair = H_SB // 2
    qpos = qt * nq + lax.broadcasted_iota(jnp.int32, (nq, 1), 0)
    qpos_rows = _rep_rows(qpos, 2)
    kcol = lax.broadcasted_iota(jnp.int32, (1, KEY_TILE), 1)
    r_ref[...] = jnp.zeros(r_ref.shape, F32)
    acc_ref[...] = jnp.zeros(acc_ref.shape, F32)

    def cond(c):
        return jnp.logical_and(c[0] <= qt, c[1] > SB_EXIT)

    def body(c):
        kt = qt - c[0]
        kpos = kt * KEY_TILE + kcol
        jobs = []
        for p in range(n_pair):
            qs = jnp.concatenate([q_ref[:, (2 * p) * LANES:(2 * p + 1) * LANES],
                                  q_ref[:, (2 * p + 1) * LANES:(2 * p + 2) * LANES]], axis=0)
            jobs.append((qs, qpos_rows, (r_ref[p], acc_ref[p]),
                         [(kv_ref[kt, p * LANES:(p + 1) * LANES, :],
                           kv_ref[kt, SB_W + p * LANES:SB_W + (p + 1) * LANES, :], kpos)]))
        r_max = None
        for p, (r, acc) in enumerate(_sb_tiles(jobs, uu_ref[...])):
            r_ref[p] = r
            acc_ref[p] = acc
            r_max = r if r_max is None else jnp.maximum(r_max, r)
        return c[0] + 1, jnp.max(r_max)

    lax.while_loop(cond, body, (jnp.int32(0), jnp.float32(0.0)))
    low = _lane_lt((nq, LANES), HEAD_DIM)
    for p in range(n_pair):
        acc = acc_ref[p]
        o_ref[:, p * LANES:(p + 1) * LANES] = jnp.where(low, acc[:nq], acc[nq:]).astype(BF16)


def _sb_prompt(q_sb_pad, sb_tiles, uu, batch, seq):
    nqt = seq // Q_TILE
    return pl.pallas_call(
        _sb_prompt_kernel,
        grid=(batch, nqt),
        in_specs=[pl.BlockSpec((Q_TILE, H_SB * LANES), lambda b, t: (b * nqt + t, 0)),
                  pl.BlockSpec((None,) + sb_tiles.shape[1:], lambda b, t: (b, 0, 0, 0)),
                  pl.BlockSpec(uu.shape, lambda b, t: (0, 0))],
        out_specs=pl.BlockSpec((Q_TILE, SB_W), lambda b, t: (b * nqt + t, 0)),
        out_shape=jax.ShapeDtypeStruct((batch * seq, SB_W), BF16),
        scratch_shapes=[pltpu.VMEM((H_SB // 2, 2 * Q_TILE, LANES), F32),
                        pltpu.VMEM((H_SB // 2, 2 * Q_TILE, LANES), F32)],
        compiler_params=_cparams(2),
        name="sb_prompt",
    )(q_sb_pad, sb_tiles, uu)


def _pos_rows(kpos):
    row = lax.broadcasted_iota(jnp.int32, (HEAD_DIM, kpos.shape[1]), 0)
    hi = jnp.right_shift(kpos, 8).astype(F32)
    lo = jnp.bitwise_and(kpos, 255).astype(F32)
    return jnp.where(row == 0, hi, jnp.where(row == 1, lo, 0.0)).astype(BF16)


def _nsa_prompt_kernel(q_ref, gate_ref, ht_ref, slc_ref, win_ref, mblk_ref, e_ref, o_ref,
                       qsel_ref, qwin_ref, part_ref, m_ref, acc_ref, *, n_sb):
    qt = pl.program_id(1)
    nq = Q_TILE
    qpos = qt * nq + lax.broadcasted_iota(jnp.int32, (nq, 1), 0)
    qpos_rows = _rep_rows(qpos, HPG)
    kcol = lax.broadcasted_iota(jnp.int32, (1, KEY_TILE), 1)
    gate = gate_ref[...]
    cmpk, cmpv = _cmp_tables(ht_ref[...])
    sel = []
    for g in range(G_NSA):
        heads = [g * HPG + h for h in range(HPG)]
        slots = [q_ref[:, h * LANES:(h + 1) * LANES] for h in heads]
        qs = _query_rows(slots, heads)
        qwin_ref[g] = qs
        p, o_c = _cmp_branch(qs, cmpk[g], cmpv, qpos_rows)
        part_ref[g] = _gate_rows(gate, g, 0, nq) * o_c
        imp = p[0:nq] + p[nq:2 * nq] + p[2 * nq:3 * nq] + p[3 * nq:4 * nq]
        sel.append(_select_blocks(imp, mblk_ref[...], qpos, n_sb, blocks_on_sublanes=True))
        qsel_ref[g] = _query_rows(slots, heads, extra=1.0 - sel[g])
    any_sel = jnp.max(jnp.maximum(sel[0], sel[1]), axis=0, keepdims=True)
    tile_of_block = jnp.right_shift(lax.broadcasted_iota(jnp.int32, any_sel.shape, 1),
                                    (KEY_TILE // SEL_BLOCK).bit_length() - 1)
    m_ref[...] = jnp.full(m_ref.shape, NEG, F32)
    acc_ref[...] = jnp.zeros(acc_ref.shape, F32)

    def tile_update(branch, kt, mask):
        tile_ref, q_scr = (slc_ref, qsel_ref) if branch == 0 else (win_ref, qwin_ref)
        pos = _pos_rows(kt * KEY_TILE + kcol)
        jobs = []
        for g in range(G_NSA):
            s = branch * G_NSA + g
            k_tile = _key_tile(tile_ref[kt, :KV_W, :], g, pos, e_ref[kt] if branch == 0 else None)
            jobs.append((q_scr[g], k_tile, _value_tile(tile_ref[kt, KV_W:, :], g), (m_ref[s], acc_ref[s])))
        for g, (m, acc) in enumerate(_softmax_tiles(jobs, mask)):
            s = branch * G_NSA + g
            m_ref[s] = m
            acc_ref[s] = acc

    def slc_body(kt, c):
        @pl.when(jnp.max(jnp.where(tile_of_block == kt, any_sel, 0.0)) > 0.5)
        def _():
            tile_update(0, kt, None)

        return c

    lax.fori_loop(0, qt, slc_body, 0)
    causal = (qt * KEY_TILE + kcol) <= qpos_rows
    tile_update(0, qt, causal)

    n_back = WINDOW // KEY_TILE
    for back in range(n_back, 0, -1):
        @pl.when(qt >= back)
        def _(back=back):
            kt = qt - back
            inside = (kt * KEY_TILE + kcol) > qpos_rows - WINDOW
            tile_update(1, kt, inside if back == n_back else None)

    tile_update(1, qt, causal)
    outs = [part_ref[g] + _gate_rows(gate, g, 1, nq) * _softmax_finish(acc_ref[g], g)
            + _gate_rows(gate, g, 2, nq) * _softmax_finish(acc_ref[G_NSA + g], g) for g in range(G_NSA)]
    low = _lane_lt((nq, LANES), HEAD_DIM)
    for h in range(HPG):
        o = jnp.where(low, outs[0][h * nq:(h + 1) * nq], outs[1][h * nq:(h + 1) * nq])
        o_ref[:, h * LANES:(h + 1) * LANES] = o.astype(BF16)


def _nsa_prompt(q_n_pad, gate, ht, slc_tiles, win_tiles, m_blk, e_tiles, batch, seq):
    nqt = seq // Q_TILE
    n_chunk = seq // CMP_STRIDE
    seq_blk = lambda a: pl.BlockSpec((None,) + a.shape[1:], lambda b, t: (b,) + (0,) * (a.ndim - 1))
    return pl.pallas_call(
        functools.partial(_nsa_prompt_kernel, n_sb=seq // SEL_BLOCK),
        grid=(batch, nqt),
        in_specs=[pl.BlockSpec((Q_TILE, H_NSA * LANES), lambda b, t: (b * nqt + t, 0)),
                  pl.BlockSpec((Q_TILE, LANES), lambda b, t: (b * nqt + t, 0)),
                  pl.BlockSpec((None, n_chunk, 4 * KV_W), lambda b, t: (b, 0, 0)),
                  seq_blk(slc_tiles), seq_blk(win_tiles),
                  pl.BlockSpec(m_blk.shape, lambda b, t: (0, 0)),
                  pl.BlockSpec(e_tiles.shape, lambda b, t: (0, 0, 0))],
        out_specs=pl.BlockSpec((Q_TILE, NSA_QW), lambda b, t: (b * nqt + t, 0)),
        out_shape=jax.ShapeDtypeStruct((batch * seq, NSA_QW), BF16),
        scratch_shapes=[pltpu.VMEM((G_NSA, HPG * Q_TILE, LANES + m_blk.shape[1]), BF16),
                        pltpu.VMEM((G_NSA, HPG * Q_TILE, LANES), BF16),
                        pltpu.VMEM((G_NSA, HPG * Q_TILE, LANES), F32),
                        pltpu.VMEM((2 * G_NSA, HPG * Q_TILE, LANES), F32),
                        pltpu.VMEM((2 * G_NSA, HPG * Q_TILE, LANES), F32)],
        compiler_params=_cparams(2),
        name="nsa_prompt",
    )(q_n_pad, gate, ht, slc_tiles, win_tiles, m_blk, e_tiles)


def _new_key_tiles(x):
    padded = jnp.concatenate([x, jnp.zeros((KEY_TILE - Q_PAD, x.shape[1]), F32)], axis=0)
    cols = [padded[:, c * LANES:(c + 1) * LANES].T for c in range(x.shape[1] // LANES)]
    return jnp.concatenate(cols, axis=0).astype(BF16)


def _nsa_sample_local_kernel(q_ref, gate_ref, ht_ref, win_ref, neww_ref, mblk_ref,
                             part_ref, sel_ref, *, past, n_sb):
    nq = Q_PAD
    rows = HPG * nq
    qpos = past + lax.broadcasted_iota(jnp.int32, (nq, 1), 0)
    qpos_rows = _rep_rows(qpos, HPG)
    kcol = lax.broadcasted_iota(jnp.int32, (1, KEY_TILE), 1)
    gate = gate_ref[...]
    cmpk, cmpv = _cmp_tables(ht_ref[...])
    n_w = win_ref.shape[1]
    qs, o_c = [], []
    for g in range(G_NSA):
        heads = [g * HPG + h for h in range(HPG)]
        qs.append(_query_rows([q_ref[h * nq:(h + 1) * nq, :] for h in heads], heads))
        p, oc = _cmp_branch(qs[g], cmpk[g], cmpv, qpos_rows)
        o_c.append(oc)
        imp = p[0:nq] + p[nq:2 * nq] + p[2 * nq:3 * nq] + p[3 * nq:4 * nq]
        sel_ref[g * nq:(g + 1) * nq, :] = _select_blocks(imp, mblk_ref[...], qpos, n_sb,
                                                         blocks_on_sublanes=False)
    tiles = [(win_ref[:, t * KEY_TILE:(t + 1) * KEY_TILE].astype(BF16), (past - n_w + t * KEY_TILE) + kcol)
             for t in range(n_w // KEY_TILE)]
    tiles.append((_new_key_tiles(neww_ref[...]), past + kcol))
    carry = [_softmax_init(rows) for _ in range(G_NSA)]
    for tile, kpos in tiles:
        dist = qpos_rows - kpos
        mask = jnp.logical_and(jnp.logical_and(dist >= 0, dist < WINDOW), kpos >= 0)
        pos = _pos_rows(kpos)
        carry = _softmax_tiles([(qs[g], _key_tile(tile[:KV_W], g, pos), _value_tile(tile[KV_W:], g), carry[g])
                                for g in range(G_NSA)], mask)
    for g in range(G_NSA):
        part_ref[g * rows:(g + 1) * rows, :] = (_gate_rows(gate, g, 0, nq) * o_c[g]
                                                + _gate_rows(gate, g, 2, nq) * _softmax_finish(carry[g][1], g))


def _nsa_sample_local(q_n_rows, gate8, ht, win_state_t, new_win8, m_blk, past, n_sb):
    n_seq = q_n_rows.shape[0]
    blk = lambda a: pl.BlockSpec((None,) + a.shape[1:], lambda b: (b,) + (0,) * (a.ndim - 1))
    nbp = m_blk.shape[1]
    return pl.pallas_call(
        functools.partial(_nsa_sample_local_kernel, past=past, n_sb=n_sb),
        grid=(n_seq,),
        in_specs=[blk(q_n_rows), blk(gate8), blk(ht), blk(win_state_t), blk(new_win8),
                  pl.BlockSpec(m_blk.shape, lambda b: (0, 0))],
        out_specs=[pl.BlockSpec((None, H_NSA * Q_PAD, LANES), lambda b: (b, 0, 0)),
                   pl.BlockSpec((None, G_NSA * Q_PAD, nbp), lambda b: (b, 0, 0))],
        out_shape=[jax.ShapeDtypeStruct((n_seq, H_NSA * Q_PAD, LANES), F32),
                   jax.ShapeDtypeStruct((n_seq, G_NSA * Q_PAD, nbp), F32)],
        compiler_params=_cparams(1),
        name="nsa_sample_local",
    )(q_n_rows, gate8, ht, win_state_t, new_win8, m_blk)


def _sb_sample_kernel(*refs, per_step, top_page, past, first_phase):
    live_ref = refs[1]
    pages = refs[2:2 + per_step]
    qsb_ref, newsb_ref, rin_ref, accin_ref, uu_ref, r_ref, acc_ref, on_ref = refs[2 + per_step:]
    b = pl.program_id(0)
    j = pl.program_id(1)
    nq = Q_PAD
    n_pair = H_SB // 2
    qpos2 = _rep_rows(past + lax.broadcasted_iota(jnp.int32, (nq, 1), 0), 2)
    kcol = lax.broadcasted_iota(jnp.int32, (1, KEY_TILE), 1)

    def visit(tiles):
        jobs = [(qsb_ref[pair * 2 * nq:(pair + 1) * 2 * nq, :], qpos2, (r_ref[pair], acc_ref[pair]),
                 [(tile[pair * LANES:(pair + 1) * LANES], tile[SB_W + pair * LANES:SB_W + (pair + 1) * LANES],
                   kpos) for tile, kpos in tiles]) for pair in range(n_pair)]
        r_max = None
        for pair, (r, acc) in enumerate(_sb_tiles(jobs, uu_ref[...])):
            r_ref[pair] = r
            acc_ref[pair] = acc
            r_max = r if r_max is None else jnp.maximum(r_max, r)
        on_ref[0] = (jnp.max(r_max) > SB_EXIT).astype(jnp.int32)

    @pl.when(j == 0)
    def _():
        r_ref[...] = rin_ref[...]
        acc_ref[...] = accin_ref[...]
        if first_phase:
            visit([(_new_key_tiles(newsb_ref[...]), past + kcol)])
        else:
            on_ref[0] = live_ref[b]

    group = 2
    for k0 in range(0, per_step, group):
        @pl.when(on_ref[0] != 0)
        def _(k0=k0):
            tiles = []
            for k in range(k0, min(k0 + group, per_step)):
                pg = top_page - (j * per_step + k)
                tiles.append((pages[k][...].astype(BF16), pg * KEY_TILE + kcol))
            visit(tiles)


def _sb_sample_phase(pages_flat, live, cache_sb_t, q_sb_rows, new_sb8, r_in, acc_in, uu, *, n_pages,
                     top_page, n_visit, per_step, past, first_phase):
    n_seq = q_sb_rows.shape[0]
    n_steps = n_visit // per_step

    def page_spec(k):
        def index(b, j, pt, lv):
            own = pt[b * n_pages + (top_page - (j * per_step + k))]
            return (jnp.where(lv[b] != 0, own, pt[0]), 0, 0)
        return pl.BlockSpec((None, 2 * SB_W, KEY_TILE), index)

    seq_blk = lambda a: pl.BlockSpec((None,) + a.shape[1:], lambda b, j, pt, lv: (b,) + (0,) * (a.ndim - 1))
    grid_spec = pltpu.PrefetchScalarGridSpec(
        num_scalar_prefetch=2,
        grid=(n_seq, n_steps),
        in_specs=[page_spec(k) for k in range(per_step)]
        + [seq_blk(q_sb_rows), seq_blk(new_sb8), seq_blk(r_in), seq_blk(acc_in),
           pl.BlockSpec(uu.shape, lambda b, j, pt, lv: (0, 0))],
        out_specs=[seq_blk(r_in), seq_blk(acc_in)],
        scratch_shapes=[pltpu.SMEM((1,), jnp.int32)],
    )
    return pl.pallas_call(
        functools.partial(_sb_sample_kernel, per_step=per_step, top_page=top_page, past=past,
                          first_phase=first_phase),
        grid_spec=grid_spec,
        out_shape=[jax.ShapeDtypeStruct(r_in.shape, F32), jax.ShapeDtypeStruct(acc_in.shape, F32)],
        compiler_params=_cparams(2),
        name="sb_sample_first" if first_phase else "sb_sample_rest",
    )(pages_flat, live, *([cache_sb_t] * per_step), q_sb_rows, new_sb8, r_in, acc_in, uu)


def _largest_divisor(n, cap):
    return max(d for d in range(1, cap + 1) if n % d == 0)


def _sb_sample(pages_flat, cache_sb_t, q_sb_rows, new_sb8, uu, n_pages, past):
    n_seq = q_sb_rows.shape[0]
    n_pair = H_SB // 2
    n_first = _largest_divisor(n_pages, 4)
    zeros_r = jnp.zeros((n_seq, n_pair, 2 * Q_PAD, LANES), F32)
    zeros_acc = jnp.zeros((n_seq, n_pair, 2 * Q_PAD, LANES), F32)
    everyone = jnp.ones((n_seq,), jnp.int32)
    common = dict(n_pages=n_pages, past=past)
    r, acc = _sb_sample_phase(pages_flat, everyone, cache_sb_t, q_sb_rows, new_sb8, zeros_r, zeros_acc, uu,
                              top_page=n_pages - 1, n_visit=n_first, per_step=n_first, first_phase=True, **common)
    n_rest = n_pages - n_first
    if n_rest:
        live = (jnp.max(r, axis=(1, 2, 3)) > SB_EXIT).astype(jnp.int32)
        r, acc = _sb_sample_phase(pages_flat, live, cache_sb_t, q_sb_rows, new_sb8, r, acc, uu,
                                  top_page=n_pages - 1 - n_first, n_visit=n_rest,
                                  per_step=_largest_divisor(n_rest, 15), first_phase=False, **common)
    low = jnp.arange(LANES) < HEAD_DIM
    o = jnp.where(low, acc[:, :, :Q_PAD], acc[:, :, Q_PAD:])
    return o.transpose(0, 2, 1, 3).reshape(n_seq, Q_PAD, SB_W).astype(BF16)


def _slc_sample_kernel(*refs, per_step, n_pages, past):
    need_ref = refs[1]
    nsa_pages = refs[2:2 + per_step]
    (qn_ref, newslc_ref, sel_ref, part_ref, gate_ref, e_ref, onsa_ref, m_ref, acc_ref) = refs[2 + per_step:]
    b = pl.program_id(0)
    j = pl.program_id(1)
    nq = Q_PAD
    rows = HPG * nq
    qpos4 = _rep_rows(past + lax.broadcasted_iota(jnp.int32, (nq, 1), 0), HPG)
    kcol = lax.broadcasted_iota(jnp.int32, (1, KEY_TILE), 1)

    def visit(tile, e_tile, kpos, mask):
        pos = _pos_rows(kpos)
        jobs = []
        for g in range(G_NSA):
            heads = [g * HPG + h for h in range(HPG)]
            not_sel = 1.0 - sel_ref[g * nq:(g + 1) * nq, :]
            q = _query_rows([qn_ref[h * nq:(h + 1) * nq, :] for h in heads], heads, extra=not_sel)
            jobs.append((q, _key_tile(tile[:KV_W], g, pos, e_tile), _value_tile(tile[KV_W:], g),
                         (m_ref[g], acc_ref[g])))
        for g, (m, acc) in enumerate(_softmax_tiles(jobs, mask)):
            m_ref[g] = m
            acc_ref[g] = acc

    @pl.when(j == 0)
    def _():
        m_ref[...] = jnp.full(m_ref.shape, NEG, F32)
        acc_ref[...] = jnp.zeros(acc_ref.shape, F32)
        visit(_new_key_tiles(newslc_ref[...]), e_ref[n_pages], past + kcol, (past + kcol) <= qpos4)

    for k in range(per_step):
        pg = n_pages - 1 - (j * per_step + k)

        @pl.when(need_ref[b * n_pages + pg] != 0)
        def _(k=k, pg=pg):
            visit(nsa_pages[k][...].astype(BF16), e_ref[pg], pg * KEY_TILE + kcol, None)

    @pl.when(j == pl.num_programs(1) - 1)
    def _():
        low = _lane_lt((nq, LANES), HEAD_DIM)
        gate = gate_ref[...]
        outs = [part_ref[g * rows:(g + 1) * rows, :] + _gate_rows(gate, g, 1, nq) * _softmax_finish(acc_ref[g], g)
                for g in range(G_NSA)]
        for h in range(HPG):
            o = jnp.where(low, outs[0][h * nq:(h + 1) * nq], outs[1][h * nq:(h + 1) * nq])
            onsa_ref[:, h * LANES:(h + 1) * LANES] = o.astype(BF16)


def _slc_sample(pages_flat, need, cache_nsa_t, q_n_rows, new_slc8, sel, part, gate8, e_tiles, n_pages, past):
    n_seq = q_n_rows.shape[0]
    per_step = _largest_divisor(n_pages, 16)
    n_steps = n_pages // per_step
    by_step = need[:, ::-1].reshape(n_seq, n_steps, per_step)
    step = jnp.arange(n_steps, dtype=jnp.int32)[None, :, None]
    shown_step = jnp.maximum(lax.cummax(jnp.where(by_step != 0, step, -1), axis=1), 0)
    slot = jnp.arange(per_step, dtype=jnp.int32)[None, None, :]
    shown_page = n_pages - 1 - (shown_step * per_step + slot)
    fetch = jnp.take_along_axis(pages_flat.reshape(n_seq, n_pages), shown_page.reshape(n_seq, -1), axis=1)
    fetch_flat = fetch.reshape(-1).astype(jnp.int32)
    need_flat = need.reshape(-1).astype(jnp.int32)

    def nsa_spec(k):
        return pl.BlockSpec((None, 2 * KV_W, KEY_TILE),
                            lambda b, j, ft, nd: (ft[(b * n_steps + j) * per_step + k], 1, 0))

    seq_blk = lambda a: pl.BlockSpec((None,) + a.shape[1:], lambda b, j, ft, nd: (b,) + (0,) * (a.ndim - 1))
    grid_spec = pltpu.PrefetchScalarGridSpec(
        num_scalar_prefetch=2,
        grid=(n_seq, n_steps),
        in_specs=[nsa_spec(k) for k in range(per_step)]
        + [seq_blk(q_n_rows), seq_blk(new_slc8), seq_blk(sel), seq_blk(part), seq_blk(gate8),
           pl.BlockSpec(e_tiles.shape, lambda b, j, ft, nd: (0, 0, 0))],
        out_specs=pl.BlockSpec((None, Q_PAD, NSA_QW), lambda b, j, ft, nd: (b, 0, 0)),
        scratch_shapes=[pltpu.VMEM((G_NSA, HPG * Q_PAD, LANES), F32),
                        pltpu.VMEM((G_NSA, HPG * Q_PAD, LANES), F32)],
    )
    return pl.pallas_call(
        functools.partial(_slc_sample_kernel, per_step=per_step, n_pages=n_pages, past=past),
        grid_spec=grid_spec,
        out_shape=jax.ShapeDtypeStruct((n_seq, Q_PAD, NSA_QW), BF16),
        compiler_params=_cparams(2),
        name="slc_sample",
    )(fetch_flat, need_flat, *([cache_nsa_t] * per_step), q_n_rows, new_slc8, sel, part, gate8, e_tiles)


def _block_sum_matrix(n_chunk_pad, n_c, n_sb, nbp):
    per = SEL_BLOCK // CMP_STRIDE
    c = jnp.arange(n_chunk_pad)[:, None]
    b = jnp.arange(nbp)[None, :]
    own = (c // per == b)
    nxt = jnp.logical_and((c + 1) // per == b, c + 1 < n_sb * per)
    m = (own.astype(F32) + nxt.astype(F32)) * (c < n_c) * (b < n_sb)
    return m.astype(BF16)


def _expand_tiles(n_tiles, nbp):
    t = jnp.arange(n_tiles)[:, None, None]
    b = jnp.arange(nbp)[None, :, None]
    k = jnp.arange(KEY_TILE)[None, None, :]
    return (((t * KEY_TILE + k) // SEL_BLOCK == b) * MASKED).astype(BF16)


def _suffix_matrix():
    j = jnp.arange(KEY_TILE)[:, None]
    s = jnp.arange(KEY_TILE)[None, :]
    u = (j > s).astype(BF16)
    return jnp.concatenate([u, u], axis=0)


def _rows_per_seq(x, n_seq, n_new, slots):
    x = x.reshape(n_seq, n_new, slots, LANES)
    x = jnp.pad(x, ((0, 0), (0, Q_PAD - n_new), (0, 0), (0, 0)))
    return x.transpose(0, 2, 1, 3).reshape(n_seq, slots * Q_PAD, LANES)


def _pad_new(x, n_seq, n_new):
    x = x.reshape(n_seq, n_new, x.shape[-1])
    return jnp.pad(x, ((0, 0), (0, Q_PAD - n_new), (0, 0)))


def _token_minor(x, lead):
    n = x.ndim
    perm = tuple(range(lead)) + tuple(range(lead + 1, n)) + (lead,)
    x = x.transpose(perm)
    return x.reshape(x.shape[:lead] + (-1, x.shape[-1]))


def _token_major(x_t, lead, feat_shape):
    x = x_t.reshape(x_t.shape[:lead] + tuple(feat_shape) + (x_t.shape[-1],))
    n = x.ndim
    perm = tuple(range(lead)) + (n - 1,) + tuple(range(lead, n - 1))
    return x.transpose(perm)


def kernel(x_prompt, x_sample, cache_sb_kv, cache_nsa_kv, state_win_kv, page_table, c_prompt, c_sample,
           w_ada, b_ada, w_ff1_up, w_ff1_down, w_in, w_cmp_k, w_cmp_v, w_o, w_ff2_up, w_ff2_down,
           ln_g, ln_b):
    batch, seq, d = x_prompt.shape
    n_seq, n_new, _ = x_sample.shape
    depth, n_pool, page = cache_sb_kv.shape[:3]
    n_pages = page_table.shape[1]
    past = n_pages * page
    n_w = state_win_kv.shape[2]
    alpha = (2.0 * depth) ** 0.25
    assert page == KEY_TILE and seq % TOKEN_TILE == 0 and n_new <= Q_PAD and n_w % KEY_TILE == 0
    assert n_new < CMP_STRIDE and past % SEL_BLOCK == 0

    uu = _suffix_matrix()
    n_chunk_p = seq // CMP_STRIDE
    n_sb_p = seq // SEL_BLOCK
    m_blk_p = _block_sum_matrix(n_chunk_p, n_chunk_p - 1, n_sb_p, LANES * pl.cdiv(n_sb_p, LANES))
    e_p = _expand_tiles(seq // KEY_TILE, m_blk_p.shape[1])
    t_all = past + n_new
    n_sb_s = pl.cdiv(t_all, SEL_BLOCK)
    n_chunk_s = past // CMP_STRIDE
    m_blk_s = _block_sum_matrix(n_chunk_s, t_all // CMP_STRIDE - 1, n_sb_s,
                                LANES * pl.cdiv(n_sb_s, LANES))
    e_s = _expand_tiles(n_pages + 1, m_blk_s.shape[1])
    blocks_per_page = KEY_TILE // SEL_BLOCK

    xp = x_prompt.reshape(batch * seq, d)
    xs = x_sample.reshape(n_seq * n_new, d)
    c_all = jnp.concatenate([c_prompt, c_sample], axis=0)
    cache_sb_t = _token_minor(cache_sb_kv, 2).reshape(depth * n_pool, 2 * SB_W, page)
    cache_nsa_t = _token_minor(cache_nsa_kv, 2).reshape(depth * n_pool, 4 * KV_W, page)
    win_state_t = _token_minor(state_win_kv, 2)
    tiles_per_seq = seq // TOKEN_TILE
    states = [[] for _ in range(6)]

    for layer in range(depth):
        mod = _ada(c_all, w_ada[layer], b_ada[layer]).reshape(batch + n_seq, N_MOD, d)
        mod_p = mod[:batch].transpose(1, 0, 2)[:, :, None, :]
        mod_s = jnp.repeat(mod[batch:], n_new, axis=0).transpose(1, 0, 2)[:, None, :, :]
        w_up1, w_dn1 = w_ff1_up[layer].astype(BF16), w_ff1_down[layer].astype(BF16)
        w_up2, w_dn2 = w_ff2_up[layer].astype(BF16), w_ff2_down[layer].astype(BF16)
        w_proj = _proj_weights(w_in[layer])
        w_kv_t = w_proj[:, _P_KVSB:].T
        w_cmp = _compress_weights(w_cmp_k[layer], w_cmp_v[layer])
        w_osb = w_o[layer, :SB_W].astype(BF16)
        w_onsa = (w_o[layer, SB_W:].reshape(G_NSA, HPG, HEAD_DIM, d).transpose(1, 0, 2, 3)
                  .reshape(NSA_QW, d).astype(BF16))
        lng, lnb = ln_g[layer], ln_b[layer]
        pages_flat = (page_table + layer * n_pool).reshape(-1).astype(jnp.int32)

        kw_p = dict(per_token=False, tiles_per_seq=tiles_per_seq)
        x1 = _ffn(xp, mod_p, w_up1, w_dn1, lng[0:1], lnb[0:1], mod_base=0, alpha=alpha, **kw_p)
        (q_sb, q_n, gate, cmp_rows, kv_sb_t, kv_nsa_t, kv_win_t, sb_tiles, slc_tiles, win_tiles) = _proj_prompt(
            x1, mod_p, w_proj, w_kv_t, batch, seq, mod_base=3)
        ht = _compress_prompt(cmp_rows, w_cmp, batch, seq)
        o_sb = _sb_prompt(q_sb, sb_tiles, uu, batch, seq)
        o_nsa = _nsa_prompt(q_n, gate, ht, slc_tiles, win_tiles, m_blk_p, e_p, batch, seq)
        xp = _ffn(x1, mod_p, w_up2, w_dn2, lng[1:3], lnb[1:3], mod_base=6, alpha=alpha,
                  oproj=(o_sb, o_nsa, w_osb, w_onsa), **kw_p)
        n_keep = min(WINDOW, seq)
        states[0].append(_token_major(kv_sb_t, 1, (2, H_SB, HEAD_DIM)))
        states[1].append(_token_major(kv_nsa_t, 1, (4, G_NSA, HEAD_DIM)))
        states[2].append(_token_major(kv_win_t[:, :, seq - n_keep:], 1, (2, G_NSA, HEAD_DIM)))

        kw_s = dict(per_token=True, tiles_per_seq=1)
        s1 = _ffn(xs, mod_s, w_up1, w_dn1, lng[0:1], lnb[0:1], mod_base=0, alpha=alpha, **kw_s)
        q_sb, q_n, gate, kv_sb, kv_nsa, kv_win = _proj_sample(s1, mod_s, w_proj, mod_base=3)
        ht = _compress_pages(cache_nsa_t, pages_flat, w_cmp, n_seq, n_pages)
        q_n_rows = _rows_per_seq(q_n, n_seq, n_new, H_NSA)
        q_sb_rows = _rows_per_seq(q_sb, n_seq, n_new, H_SB)
        gate8 = _pad_new(gate, n_seq, n_new)
        part, sel = _nsa_sample_local(q_n_rows, gate8, ht, win_state_t[layer],
                                      _pad_new(kv_win, n_seq, n_new), m_blk_s, past, n_sb_s)
        page_sel = sel[:, :, :n_pages * blocks_per_page].reshape(n_seq, -1, n_pages, blocks_per_page)
        need = (jnp.max(page_sel, axis=(1, 3)) > 0.5).astype(jnp.int32)
        o_sb = _sb_sample(pages_flat, cache_sb_t, q_sb_rows, _pad_new(kv_sb, n_seq, n_new), uu, n_pages, past)
        o_nsa = _slc_sample(pages_flat, need, cache_nsa_t, q_n_rows, _pad_new(kv_nsa[:, 2 * KV_W:], n_seq, n_new),
                            sel, part, gate8, e_s, n_pages, past)
        o_sb = o_sb[:, :n_new].reshape(n_seq * n_new, SB_W)
        o_nsa = o_nsa[:, :n_new].reshape(n_seq * n_new, NSA_QW)
        xs = _ffn(s1, mod_s, w_up2, w_dn2, lng[1:3], lnb[1:3], mod_base=6, alpha=alpha,
                  oproj=(o_sb, o_nsa, w_osb, w_onsa), **kw_s)
        new_win_t = jnp.concatenate([win_state_t[layer][:, :, n_new:],
                                     kv_win.reshape(n_seq, n_new, 2 * KV_W).transpose(0, 2, 1)], axis=2)
        states[3].append(kv_sb.reshape(n_seq, n_new, 2, H_SB, HEAD_DIM))
        states[4].append(kv_nsa.reshape(n_seq, n_new, 4, G_NSA, HEAD_DIM))
        states[5].append(_token_major(new_win_t, 1, (2, G_NSA, HEAD_DIM)))

    return (xp.reshape(batch, seq, d), xs.reshape(n_seq, n_new, d)) + tuple(jnp.stack(s) for s in states)
```

```python
import functools

import jax
import jax.numpy as jnp
from jax import lax
from jax.experimental import pallas as pl
from jax.experimental.pallas import tpu as pltpu

F32 = jnp.float32
BF16 = jnp.bfloat16

HEAD_DIM = 64
H_SB = 8
H_NSA = 8
G_NSA = 2
HPG = H_NSA // G_NSA
CMP_BLOCK = 32
CMP_STRIDE = 16
SEL_BLOCK = 64
N_SELECT = 16
WINDOW = 512
N_MOD = 9
LN_EPS = 1e-5
FORCE_BONUS = 1.0e4
NEG = -1.0e30
MASKED = -2.0e30

SB_W = H_SB * HEAD_DIM
NSA_QW = H_NSA * HEAD_DIM
KV_W = G_NSA * HEAD_DIM
OFF_NSA_Q = 3 * SB_W
OFF_NSA_KV = OFF_NSA_Q + NSA_QW
OFF_WIN = OFF_NSA_KV + 4 * KV_W
OFF_GATE = OFF_WIN + 2 * KV_W
N_IN = OFF_GATE + 3 * H_NSA

LANES = 128
KEY_TILE = 128
VMEM_LIMIT = 56 * 1024 * 1024

TOKEN_TILE = 512
Q_TILE = 128
FF_CHUNK = 256
Q_PAD = 8
SB_EXIT = -88.0


def _cparams(n_axes):
    return pltpu.CompilerParams(dimension_semantics=("arbitrary",) * n_axes,
                                vmem_limit_bytes=VMEM_LIMIT)


def _dot(a, b):
    return jnp.dot(a, b, preferred_element_type=F32)


def _dot_nt(a, b):
    return lax.dot_general(a, b, (((1,), (1,)), ((), ())), preferred_element_type=F32)


def _split_bf16(x, parts):
    out = []
    r = x
    for _ in range(parts):
        h = r.astype(BF16)
        out.append(h)
        r = r - h.astype(F32)
    return out


def _sigmoid(x):
    return 1.0 / (1.0 + jnp.exp(-x))


def _layer_norm(y, g, b):
    mu = jnp.mean(y, axis=-1, keepdims=True)
    d = y - mu
    var = jnp.mean(d * d, axis=-1, keepdims=True)
    return d * lax.rsqrt(var + LN_EPS) * g + b


def _ada_kernel(c_ref, w_ref, b_ref, o_ref):
    c = c_ref[...]
    a = c * _sigmoid(c)
    a_hi, a_lo = _split_bf16(a, 2)
    w_hi, w_lo = _split_bf16(w_ref[...], 2)
    o_ref[...] = _dot(a_hi, w_hi) + (_dot(a_hi, w_lo) + _dot(a_lo, w_hi)) + b_ref[...]


def _ada(c, w, b):
    n, d = c.shape
    n_out = w.shape[1]
    bn = 1024
    return pl.pallas_call(
        _ada_kernel,
        grid=(n_out // bn,),
        in_specs=[pl.BlockSpec((n, d), lambda j: (0, 0)),
                  pl.BlockSpec((d, bn), lambda j: (0, j)),
                  pl.BlockSpec((1, bn), lambda j: (0, j))],
        out_specs=pl.BlockSpec((n, bn), lambda j: (0, j)),
        out_shape=jax.ShapeDtypeStruct((n, n_out), F32),
        compiler_params=_cparams(1),
        name="ada",
    )(c, w, b.reshape(1, n_out))


def _mod_spec(per_token, tm, d, tiles_per_seq):
    if per_token:
        return pl.BlockSpec((N_MOD, None, tm, d), lambda i: (0, 0, i, 0))
    return pl.BlockSpec((N_MOD, None, 1, d), lambda i: (0, i // tiles_per_seq, 0, 0))


def _ffn_kernel(*refs, mod_base, d_ff, alpha, fuse_oproj):
    if fuse_oproj:
        (x_ref, osb_ref, onsa_ref, mod_ref, wosb_ref, wonsa_ref, wup_ref, wdn_ref,
         lng_ref, lnb_ref, o_ref, acc_ref) = refs
        mixed = _dot(osb_ref[...], wosb_ref[...]) + _dot(onsa_ref[...], wonsa_ref[...])
        x = _layer_norm(alpha * x_ref[...] + mod_ref[mod_base - 1] * mixed,
                        lng_ref[0:1, :], lnb_ref[0:1, :])
        ln_row = 1
    else:
        x_ref, mod_ref, wup_ref, wdn_ref, lng_ref, lnb_ref, o_ref, acc_ref = refs
        x = x_ref[...]
        ln_row = 0
    u = (x * (1.0 + mod_ref[mod_base + 1]) + mod_ref[mod_base]).astype(BF16)
    for c in range(d_ff // FF_CHUNK):
        lo = c * FF_CHUNK
        g = _dot(u, wup_ref[:, lo:lo + FF_CHUNK])
        v = _dot(u, wup_ref[:, d_ff + lo:d_ff + lo + FF_CHUNK])
        act = (g * _sigmoid(g) * v).astype(BF16)
        part = _dot(act, wdn_ref[lo:lo + FF_CHUNK, :])
        if c == 0:
            acc_ref[...] = part
        else:
            acc_ref[...] += part
    y = alpha * x + 0.5 * mod_ref[mod_base + 2] * acc_ref[...]
    o_ref[...] = _layer_norm(y, lng_ref[ln_row:ln_row + 1, :], lnb_ref[ln_row:ln_row + 1, :])


def _ffn(x, mod4, w_up, w_down, ln_g2, ln_b2, *, mod_base, alpha, per_token, tiles_per_seq,
         oproj=None):
    t, d = x.shape
    d_ff = w_down.shape[0]
    tm = min(TOKEN_TILE, t)
    row = lambda i: (i, 0)
    const = lambda i: (0, 0)
    in_specs = [pl.BlockSpec((tm, d), row)]
    args = [x]
    if oproj is not None:
        o_sb, o_nsa, w_osb, w_onsa = oproj
        in_specs += [pl.BlockSpec((tm, SB_W), row), pl.BlockSpec((tm, NSA_QW), row)]
        args += [o_sb, o_nsa]
    in_specs.append(_mod_spec(per_token, tm, d, tiles_per_seq))
    args.append(mod4)
    if oproj is not None:
        in_specs += [pl.BlockSpec((SB_W, d), const), pl.BlockSpec((NSA_QW, d), const)]
        args += [w_osb, w_onsa]
    in_specs += [pl.BlockSpec((d, 2 * d_ff), const), pl.BlockSpec((d_ff, d), const),
                 pl.BlockSpec(ln_g2.shape, const), pl.BlockSpec(ln_b2.shape, const)]
    args += [w_up, w_down, ln_g2, ln_b2]
    return pl.pallas_call(
        functools.partial(_ffn_kernel, mod_base=mod_base, d_ff=d_ff, alpha=alpha,
                          fuse_oproj=oproj is not None),
        grid=(t // tm,),
        in_specs=in_specs,
        out_specs=pl.BlockSpec((tm, d), row),
        out_shape=jax.ShapeDtypeStruct((t, d), F32),
        scratch_shapes=[pltpu.VMEM((tm, d), F32)],
        compiler_params=_cparams(1),
        name="ffn_oproj" if oproj is not None else "ffn",
    )(*args)


_P_QSB = 0
_P_QN = _P_QSB + H_SB * LANES
_P_GATE = _P_QN + H_NSA * LANES
_P_KVSB = _P_GATE + LANES
_P_KVN = _P_KVSB + 2 * SB_W
_P_WIN = _P_KVN + 4 * KV_W
_P_END = _P_WIN + 2 * KV_W
_KV_ROWS = _P_END - _P_KVSB


def _proj_weights(w_in):
    d = w_in.shape[0]
    scale = HEAD_DIM ** -0.5
    q_sb = (w_in[:, :SB_W] * scale).reshape(d, H_SB, HEAD_DIM)
    q_sb_pad = jnp.zeros((d, H_SB, LANES), F32)
    for h in range(H_SB):
        o = (h % 2) * HEAD_DIM
        q_sb_pad = q_sb_pad.at[:, h, o:o + HEAD_DIM].set(q_sb[:, h])
    q_n = (w_in[:, OFF_NSA_Q:OFF_NSA_KV] * scale).reshape(d, H_NSA, HEAD_DIM)
    q_n_pad = jnp.zeros((d, H_NSA, LANES), F32).at[:, :, :HEAD_DIM].set(q_n)
    gate = jnp.pad(w_in[:, OFF_GATE:], ((0, 0), (0, LANES - 3 * H_NSA)))
    w = jnp.concatenate([q_sb_pad.reshape(d, -1), q_n_pad.reshape(d, -1), gate, w_in[:, SB_W:3 * SB_W],
                         w_in[:, OFF_NSA_KV:OFF_WIN], w_in[:, OFF_WIN:OFF_GATE]], axis=1)
    return w.astype(BF16)


def _proj_rows(u, w_ref, qsb_ref, qn_ref, gate_ref):
    qsb_ref[...] = _dot(u, w_ref[:, _P_QSB:_P_QN]).astype(BF16)
    qn_ref[...] = _dot(u, w_ref[:, _P_QN:_P_GATE]).astype(BF16)
    gate_ref[...] = _sigmoid(_dot(u, w_ref[:, _P_GATE:_P_KVSB]))


def _proj_sample_kernel(x_ref, mod_ref, w_ref, qsb_ref, qn_ref, gate_ref, kvsb_ref, kvn_ref, win_ref,
                        *, mod_base):
    u = (x_ref[...] * (1.0 + mod_ref[mod_base + 1]) + mod_ref[mod_base]).astype(BF16)
    _proj_rows(u, w_ref, qsb_ref, qn_ref, gate_ref)
    kvsb_ref[...] = _dot(u, w_ref[:, _P_KVSB:_P_KVN])
    kvn_ref[...] = _dot(u, w_ref[:, _P_KVN:_P_WIN])
    win_ref[...] = _dot(u, w_ref[:, _P_WIN:_P_END])


def _proj_sample(x, mod4, w_proj, *, mod_base):
    t, d = x.shape
    row = lambda i: (i, 0)
    widths = [(H_SB * LANES, BF16), (H_NSA * LANES, BF16), (LANES, F32), (2 * SB_W, F32),
              (4 * KV_W, F32), (2 * KV_W, F32)]
    return pl.pallas_call(
        functools.partial(_proj_sample_kernel, mod_base=mod_base),
        grid=(1,),
        in_specs=[pl.BlockSpec((t, d), row), _mod_spec(True, t, d, 1),
                  pl.BlockSpec(w_proj.shape, lambda i: (0, 0))],
        out_specs=[pl.BlockSpec((t, w), row) for w, _ in widths],
        out_shape=[jax.ShapeDtypeStruct((t, w), dt) for w, dt in widths],
        compiler_params=_cparams(1),
        name="proj_sample",
    )(x, mod4, w_proj)


def _proj_prompt_kernel(x_ref, mod_ref, w_ref, wt_ref, qsb_ref, qn_ref, gate_ref, cmp_ref,
                        sbt_ref, nsat_ref, wint_ref, sbtile_ref, slctile_ref, wintile_ref, *, mod_base):
    u = (x_ref[...] * (1.0 + mod_ref[mod_base + 1]) + mod_ref[mod_base]).astype(BF16)
    _proj_rows(u, w_ref, qsb_ref, qn_ref, gate_ref)
    cmp_ref[...] = _dot(u, w_ref[:, _P_KVN:_P_KVN + 2 * KV_W])
    kvt = _dot_nt(wt_ref[...], u)
    n_sb, n_nsa = 2 * SB_W, 4 * KV_W
    sbt_ref[...] = kvt[:n_sb]
    nsat_ref[...] = kvt[n_sb:n_sb + n_nsa]
    wint_ref[...] = kvt[n_sb + n_nsa:]
    kvb = kvt.astype(BF16)
    for j in range(sbtile_ref.shape[0]):
        cols = slice(j * KEY_TILE, (j + 1) * KEY_TILE)
        sbtile_ref[j] = kvb[:n_sb, cols]
        slctile_ref[j] = kvb[n_sb + 2 * KV_W:n_sb + n_nsa, cols]
        wintile_ref[j] = kvb[n_sb + n_nsa:, cols]


def _proj_prompt(x, mod4, w_proj, w_kv_t, batch, seq, *, mod_base):
    t, d = x.shape
    tm = TOKEN_TILE
    tps = seq // tm
    kt = tm // KEY_TILE
    nkt = seq // KEY_TILE
    row = lambda i: (i, 0)
    tr = lambda i: (i // tps, 0, i % tps)
    tile = lambda i: (i // tps, i % tps, 0, 0)
    rows = [(H_SB * LANES, BF16), (H_NSA * LANES, BF16), (LANES, F32), (2 * KV_W, F32)]
    trans = [2 * SB_W, 4 * KV_W, 2 * KV_W]
    tiles = [2 * SB_W, 2 * KV_W, 2 * KV_W]
    return pl.pallas_call(
        functools.partial(_proj_prompt_kernel, mod_base=mod_base),
        grid=(t // tm,),
        in_specs=[pl.BlockSpec((tm, d), row), _mod_spec(False, tm, d, tps),
                  pl.BlockSpec(w_proj.shape, lambda i: (0, 0)),
                  pl.BlockSpec(w_kv_t.shape, lambda i: (0, 0))],
        out_specs=[pl.BlockSpec((tm, w), row) for w, _ in rows]
        + [pl.BlockSpec((None, r, tm), tr) for r in trans]
        + [pl.BlockSpec((None, kt, r, KEY_TILE), tile) for r in tiles],
        out_shape=[jax.ShapeDtypeStruct((t, w), dt) for w, dt in rows]
        + [jax.ShapeDtypeStruct((batch, r, seq), F32) for r in trans]
        + [jax.ShapeDtypeStruct((batch, nkt, r, KEY_TILE), BF16) for r in tiles],
        compiler_params=_cparams(1),
        name="proj_prompt",
    )(x, mod4, w_proj, w_kv_t)


def _compress_weights(w_cmp_k, w_cmp_v):
    n_half = CMP_BLOCK // CMP_STRIDE
    kinds = []
    for w in (w_cmp_k, w_cmp_v):
        w = w.reshape(n_half, CMP_STRIDE, HEAD_DIM, HEAD_DIM)
        blk = jnp.zeros((n_half, CMP_STRIDE, KV_W, KV_W), F32)
        for g in range(G_NSA):
            lo = g * HEAD_DIM
            blk = blk.at[:, :, lo:lo + HEAD_DIM, lo:lo + HEAD_DIM].set(w)
        kinds.append(jnp.concatenate([blk[0], blk[1]], axis=-1).reshape(CMP_STRIDE * KV_W, 2 * KV_W))
    return jnp.stack(kinds, axis=0).astype(BF16)


def _compress_rows(xk_ref, xv_ref, w_ref, o_ref):
    n_chunk = o_ref.shape[0]
    for kind, x_ref in enumerate((xk_ref, xv_ref)):
        chunk_rows = jnp.concatenate([x_ref[pl.ds(i, n_chunk, stride=CMP_STRIDE), :].astype(BF16)
                                      for i in range(CMP_STRIDE)], axis=1)
        acc = _dot(chunk_rows, w_ref[kind])
        o_ref[:, kind * KV_W:(kind + 1) * KV_W] = acc[:, :KV_W]
        o_ref[:, (2 + kind) * KV_W:(3 + kind) * KV_W] = acc[:, KV_W:]


def _compress_prompt_kernel(xk_ref, xv_ref, w_ref, o_ref):
    _compress_rows(xk_ref, xv_ref, w_ref, o_ref)


def _compress_pages_kernel(*refs, per_step):
    page_refs = refs[1:1 + per_step]
    w_ref, o_ref, xk_ref, xv_ref = refs[1 + per_step:]
    eye = jnp.where(lax.broadcasted_iota(jnp.int32, (KEY_TILE, KEY_TILE), 0)
                    == lax.broadcasted_iota(jnp.int32, (KEY_TILE, KEY_TILE), 1), 1.0, 0.0).astype(BF16)
    for k, p_ref in enumerate(page_refs):
        rows = slice(k * KEY_TILE, (k + 1) * KEY_TILE)
        xk_ref[rows, :] = _dot_nt(eye, p_ref[:KV_W, :].astype(BF16))
        xv_ref[rows, :] = _dot_nt(eye, p_ref[KV_W:, :].astype(BF16))
    _compress_rows(xk_ref, xv_ref, w_ref, o_ref)


def _compress_prompt(kv_nsa, w_cmp, batch, seq):
    n_chunk = seq // CMP_STRIDE
    return pl.pallas_call(
        _compress_prompt_kernel,
        grid=(batch,),
        in_specs=[pl.BlockSpec((seq, KV_W), lambda b: (b, 0)),
                  pl.BlockSpec((seq, KV_W), lambda b: (b, 1)),
                  pl.BlockSpec(w_cmp.shape, lambda b: (0, 0, 0))],
        out_specs=pl.BlockSpec((None, n_chunk, 4 * KV_W), lambda b: (b, 0, 0)),
        out_shape=jax.ShapeDtypeStruct((batch, n_chunk, 4 * KV_W), F32),
        compiler_params=_cparams(1),
        name="compress_prompt",
    )(kv_nsa, kv_nsa, w_cmp)


def _compress_pages(cache_nsa_t, pages_flat, w_cmp, n_seq, n_pages):
    per_step = min(32, n_pages)
    n_steps = n_pages // per_step
    chunks = per_step * KEY_TILE // CMP_STRIDE

    def page_spec(k):
        return pl.BlockSpec((None, 2 * KV_W, KEY_TILE),
                            lambda b, j, pt: (pt[b * n_pages + j * per_step + k], 0, 0))

    grid_spec = pltpu.PrefetchScalarGridSpec(
        num_scalar_prefetch=1,
        grid=(n_seq, n_steps),
        in_specs=[page_spec(k) for k in range(per_step)]
        + [pl.BlockSpec(w_cmp.shape, lambda b, j, pt: (0, 0, 0))],
        out_specs=pl.BlockSpec((None, chunks, 4 * KV_W), lambda b, j, pt: (b, j, 0)),
        scratch_shapes=[pltpu.VMEM((per_step * KEY_TILE, KV_W), F32),
                        pltpu.VMEM((per_step * KEY_TILE, KV_W), F32)],
    )
    return pl.pallas_call(
        functools.partial(_compress_pages_kernel, per_step=per_step),
        grid_spec=grid_spec,
        out_shape=jax.ShapeDtypeStruct((n_seq, n_steps * chunks, 4 * KV_W), F32),
        compiler_params=_cparams(2),
        name="compress_pages",
    )(pages_flat, *([cache_nsa_t] * per_step), w_cmp)


def _rep_rows(x, n):
    return jnp.concatenate([x] * n, axis=0)


def _masked_softmax(s, mask):
    s = jnp.where(mask, s, NEG)
    m = jnp.max(s, axis=-1, keepdims=True)
    e = jnp.where(mask, jnp.exp(s - m), 0.0)
    return e / jnp.maximum(jnp.sum(e, axis=-1, keepdims=True), 1e-30)


def _with_positions(x, pos):
    lane = lax.broadcasted_iota(jnp.int32, x.shape, x.ndim - 1)
    hi = jnp.right_shift(pos, 8).astype(F32)
    lo = jnp.bitwise_and(pos, 255).astype(F32)
    return jnp.where(lane == HEAD_DIM, hi, jnp.where(lane == HEAD_DIM + 1, lo, x))


def _query_rows(q_slots, heads, extra=None):
    rows = []
    for q, h in zip(q_slots, heads):
        slope = 2.0 ** -(h + 1)
        lane = lax.broadcasted_iota(jnp.int32, q.shape, 1)
        q = jnp.where(lane == HEAD_DIM, 256.0 * slope, jnp.where(lane == HEAD_DIM + 1, slope, q.astype(F32)))
        rows.append(q if extra is None else jnp.concatenate([q, extra], axis=1))
    return jnp.concatenate(rows, axis=0).astype(BF16)


def _cmp_tables(ht):
    n_chunk = ht.shape[0]
    cmp = ht[:, :2 * KV_W] + pltpu.roll(ht[:, 2 * KV_W:], n_chunk - 1, 0)
    k = cmp[:, :KV_W]
    low = _lane_lt(k.shape, HEAD_DIM)
    cend = lax.broadcasted_iota(jnp.int32, (n_chunk, 1), 0) * CMP_STRIDE + (CMP_BLOCK - 1)
    keys = [_with_positions(jnp.where(low, k if g == 0 else pltpu.roll(k, HEAD_DIM, 1), 0.0), cend).astype(BF16)
            for g in range(G_NSA)]
    return keys, cmp[:, KV_W:].astype(BF16)


def _cmp_branch(qs, cmpk, cmpv, qpos_rows):
    n_chunk = cmpk.shape[0]
    s = _dot_nt(qs, cmpk)
    cidx = lax.broadcasted_iota(jnp.int32, (1, n_chunk), 1)
    cend = cidx * CMP_STRIDE + (CMP_BLOCK - 1)
    mask = jnp.logical_and(cend <= qpos_rows, cidx < n_chunk - 1)
    p = _masked_softmax(s, mask)
    return p, _dot(p.astype(BF16), cmpv)


def _select_blocks(imp, m_blk, qpos_col, n_sb, *, blocks_on_sublanes):
    nbp = m_blk.shape[1]
    imp_blk = sum(_dot(part, m_blk) for part in _split_bf16(imp, 3))
    bidx = lax.broadcasted_iota(jnp.int32, (1, nbp), 1)
    tb = jnp.right_shift(qpos_col, SEL_BLOCK.bit_length() - 1)
    valid = jnp.logical_and(bidx * SEL_BLOCK <= qpos_col, bidx < n_sb)
    forced = jnp.logical_or(bidx == 0, jnp.logical_or(bidx == tb, bidx == tb - 1))
    score = jnp.where(valid, imp_blk + jnp.where(forced, FORCE_BONUS, 0.0), -jnp.inf)
    k_sel = min(N_SELECT, n_sb)
    if blocks_on_sublanes:
        st = score.T
        ridx = lax.broadcasted_iota(jnp.int32, (nbp, 1), 0)
        cnt = jnp.zeros(st.shape, F32)
        for bp in range(n_sb):
            row = st[bp:bp + 1, :]
            beats = jnp.logical_or(row > st, jnp.logical_and(row == st, ridx > bp))
            cnt = cnt + jnp.where(beats, 1.0, 0.0)
        few = jnp.where(cnt < k_sel, 1.0, 0.0).T
    else:
        cnt = jnp.zeros(score.shape, F32)
        for bp in range(n_sb):
            col = score[:, bp:bp + 1]
            beats = jnp.logical_or(col > score, jnp.logical_and(col == score, bidx > bp))
            cnt = cnt + jnp.where(beats, 1.0, 0.0)
        few = jnp.where(cnt < k_sel, 1.0, 0.0)
    return jnp.where(valid, few, 0.0)


def _softmax_tiles(jobs, mask):
    logits = [_dot(j[0], j[1]) for j in jobs]
    mids = []
    for s, (_, _, _, (m, _)) in zip(logits, jobs):
        if mask is not None:
            s = jnp.where(mask, s, MASKED)
        m_new = jnp.maximum(m, jnp.max(s, axis=-1, keepdims=True))
        m_wide = jnp.concatenate([m_new] * (s.shape[1] // LANES), axis=1)
        mids.append((m_new, jnp.exp(m - m_new), jnp.exp(s - m_wide).astype(BF16)))
    return [(m_new, a * j[3][1] + _dot_nt(p, j[2])) for (m_new, a, p), j in zip(mids, jobs)]


def _value_tile(vt, g):
    ones = jnp.ones((HEAD_DIM, vt.shape[1]), BF16)
    return jnp.concatenate([vt[:HEAD_DIM], ones] if g == 0 else [ones, vt[HEAD_DIM:]], axis=0)


def _key_tile(kt, g, pos_rows, mask_rows=None):
    parts = [kt[g * HEAD_DIM:(g + 1) * HEAD_DIM], pos_rows]
    if mask_rows is not None:
        parts.append(mask_rows)
    return jnp.concatenate(parts, axis=0)


def _softmax_init(rows):
    return (jnp.full((rows, LANES), NEG, F32), jnp.zeros((rows, LANES), F32))


def _softmax_finish(acc, g):
    lane = (1 - g) * HEAD_DIM
    return acc / jnp.maximum(acc[:, lane:lane + 1], 1e-30)


def _sb_tiles(jobs, uu):
    logits = [[_dot(qs, kt) for kt, _, _ in tiles] for qs, _, _, tiles in jobs]
    mids = []
    for (_, qpos_rows, _, tiles), zs in zip(jobs, logits):
        row = []
        for z, (_, _, kpos_cols) in zip(zs, tiles):
            causal = kpos_cols < qpos_rows
            softplus = jnp.maximum(z, 0.0) + jnp.log(1.0 + jnp.exp(-jnp.abs(z)))
            log_1mb = jnp.where(causal, -softplus, 0.0)
            hi, lo = _split_bf16(log_1mb, 2)
            row.append((causal, z - softplus, jnp.sum(log_1mb, axis=-1, keepdims=True),
                        jnp.concatenate([hi, lo], axis=1)))
        mids.append(row)
    between = [[_dot(m[3], uu) for m in row] for row in mids]
    weights, r_out = [], []
    for (_, _, (r, _), _), row, btw in zip(jobs, mids, between):
        ws = []
        for (causal, log_beta, total, _), b in zip(row, btw):
            ws.append(jnp.where(causal, jnp.exp(log_beta + b + r), 0.0).astype(BF16))
            r = r + total
        weights.append(ws)
        r_out.append(r)
    out = []
    for (_, _, (_, acc), tiles), ws, r in zip(jobs, weights, r_out):
        for w, (_, vt, _) in zip(ws, tiles):
            acc = acc + _dot_nt(w, vt)
        out.append((r, acc))
    return out


def _gate_rows(gate, g, j, nq):
    cols = [gate[:, (g * HPG + h) * 3 + j:(g * HPG + h) * 3 + j + 1] for h in range(HPG)]
    return jnp.concatenate(cols, axis=0)


def _lane_lt(shape, n):
    return lax.broadcasted_iota(jnp.int32, shape, 1) < n


def _sb_prompt_kernel(q_ref, kv_ref, uu_ref, o_ref, r_ref, acc_ref):
    qt = pl.program_id(1)
    nq = Q_TILE
    n_pair = H_SB // 2
    qpos = qt * nq + lax.broadcasted_iota(jnp.int32, (nq, 1), 0)
    qpos_rows = _rep_rows(qpos, 2)
    kcol = lax.broadcasted_iota(jnp.int32, (1, KEY_TILE), 1)
    r_ref[...] = jnp.zeros(r_ref.shape, F32)
    acc_ref[...] = jnp.zeros(acc_ref.shape, F32)

    def cond(c):
        return jnp.logical_and(c[0] <= qt, c[1] > SB_EXIT)

    def body(c):
        kt = qt - c[0]
        kpos = kt * KEY_TILE + kcol
        jobs = []
        for p in range(n_pair):
            qs = jnp.concatenate([q_ref[:, (2 * p) * LANES:(2 * p + 1) * LANES],
                                  q_ref[:, (2 * p + 1) * LANES:(2 * p + 2) * LANES]], axis=0)
            jobs.append((qs, qpos_rows, (r_ref[p], acc_ref[p]),
                         [(kv_ref[kt, p * LANES:(p + 1) * LANES, :],
                           kv_ref[kt, SB_W + p * LANES:SB_W + (p + 1) * LANES, :], kpos)]))
        r_max = None
        for p, (r, acc) in enumerate(_sb_tiles(jobs, uu_ref[...])):
            r_ref[p] = r
            acc_ref[p] = acc
            r_max = r if r_max is None else jnp.maximum(r_max, r)
        return c[0] + 1, jnp.max(r_max)

    lax.while_loop(cond, body, (jnp.int32(0), jnp.float32(0.0)))
    low = _lane_lt((nq, LANES), HEAD_DIM)
    for p in range(n_pair):
        acc = acc_ref[p]
        o_ref[:, p * LANES:(p + 1) * LANES] = jnp.where(low, acc[:nq], acc[nq:]).astype(BF16)


def _sb_prompt(q_sb_pad, sb_tiles, uu, batch, seq):
    nqt = seq // Q_TILE
    return pl.pallas_call(
        _sb_prompt_kernel,
        grid=(batch, nqt),
        in_specs=[pl.BlockSpec((Q_TILE, H_SB * LANES), lambda b, t: (b * nqt + t, 0)),
                  pl.BlockSpec((None,) + sb_tiles.shape[1:], lambda b, t: (b, 0, 0, 0)),
                  pl.BlockSpec(uu.shape, lambda b, t: (0, 0))],
        out_specs=pl.BlockSpec((Q_TILE, SB_W), lambda b, t: (b * nqt + t, 0)),
        out_shape=jax.ShapeDtypeStruct((batch * seq, SB_W), BF16),
        scratch_shapes=[pltpu.VMEM((H_SB // 2, 2 * Q_TILE, LANES), F32),
                        pltpu.VMEM((H_SB // 2, 2 * Q_TILE, LANES), F32)],
        compiler_params=_cparams(2),
        name="sb_prompt",
    )(q_sb_pad, sb_tiles, uu)


def _pos_rows(kpos):
    row = lax.broadcasted_iota(jnp.int32, (HEAD_DIM, kpos.shape[1]), 0)
    hi = jnp.right_shift(kpos, 8).astype(F32)
    lo = jnp.bitwise_and(kpos, 255).astype(F32)
    return jnp.where(row == 0, hi, jnp.where(row == 1, lo, 0.0)).astype(BF16)


def _nsa_prompt_kernel(q_ref, gate_ref, ht_ref, slc_ref, win_ref, mblk_ref, e_ref, o_ref,
                       qsel_ref, qwin_ref, part_ref, m_ref, acc_ref, *, n_sb):
    qt = pl.program_id(1)
    nq = Q_TILE
    qpos = qt * nq + lax.broadcasted_iota(jnp.int32, (nq, 1), 0)
    qpos_rows = _rep_rows(qpos, HPG)
    kcol = lax.broadcasted_iota(jnp.int32, (1, KEY_TILE), 1)
    gate = gate_ref[...]
    cmpk, cmpv = _cmp_tables(ht_ref[...])
    sel = []
    for g in range(G_NSA):
        heads = [g * HPG + h for h in range(HPG)]
        slots = [q_ref[:, h * LANES:(h + 1) * LANES] for h in heads]
        qs = _query_rows(slots, heads)
        qwin_ref[g] = qs
        p, o_c = _cmp_branch(qs, cmpk[g], cmpv, qpos_rows)
        part_ref[g] = _gate_rows(gate, g, 0, nq) * o_c
        imp = p[0:nq] + p[nq:2 * nq] + p[2 * nq:3 * nq] + p[3 * nq:4 * nq]
        sel.append(_select_blocks(imp, mblk_ref[...], qpos, n_sb, blocks_on_sublanes=True))
        qsel_ref[g] = _query_rows(slots, heads, extra=1.0 - sel[g])
    any_sel = jnp.max(jnp.maximum(sel[0], sel[1]), axis=0, keepdims=True)
    tile_of_block = jnp.right_shift(lax.broadcasted_iota(jnp.int32, any_sel.shape, 1),
                                    (KEY_TILE // SEL_BLOCK).bit_length() - 1)
    m_ref[...] = jnp.full(m_ref.shape, NEG, F32)
    acc_ref[...] = jnp.zeros(acc_ref.shape, F32)

    def tile_update(branches, kt, mask):
        pos = _pos_rows(kt * KEY_TILE + kcol)
        jobs, slots = [], []
        for branch in branches:
            tile_ref, q_scr = (slc_ref, qsel_ref) if branch == 0 else (win_ref, qwin_ref)
            for g in range(G_NSA):
                s = branch * G_NSA + g
                k_tile = _key_tile(tile_ref[kt, :KV_W, :], g, pos, e_ref[kt] if branch == 0 else None)
                jobs.append((q_scr[g], k_tile, _value_tile(tile_ref[kt, KV_W:, :], g), (m_ref[s], acc_ref[s])))
                slots.append(s)
        for s, (m, acc) in zip(slots, _softmax_tiles(jobs, mask)):
            m_ref[s] = m
            acc_ref[s] = acc

    def slc_body(kt, c):
        @pl.when(jnp.max(jnp.where(tile_of_block == kt, any_sel, 0.0)) > 0.5)
        def _():
            tile_update((0,), kt, None)

        return c

    lax.fori_loop(0, qt, slc_body, 0)

    n_back = WINDOW // KEY_TILE
    for back in range(n_back, 0, -1):
        @pl.when(qt >= back)
        def _(back=back):
            kt = qt - back
            inside = (kt * KEY_TILE + kcol) > qpos_rows - WINDOW
            tile_update((1,), kt, inside if back == n_back else None)

    tile_update((0, 1), qt, (qt * KEY_TILE + kcol) <= qpos_rows)
    outs = [part_ref[g] + _gate_rows(gate, g, 1, nq) * _softmax_finish(acc_ref[g], g)
            + _gate_rows(gate, g, 2, nq) * _softmax_finish(acc_ref[G_NSA + g], g) for g in range(G_NSA)]
    low = _lane_lt((nq, LANES), HEAD_DIM)
    for h in range(HPG):
        o = jnp.where(low, outs[0][h * nq:(h + 1) * nq], outs[1][h * nq:(h + 1) * nq])
        o_ref[:, h * LANES:(h + 1) * LANES] = o.astype(BF16)


def _nsa_prompt(q_n_pad, gate, ht, slc_tiles, win_tiles, m_blk, e_tiles, batch, seq):
    nqt = seq // Q_TILE
    n_chunk = seq // CMP_STRIDE
    seq_blk = lambda a: pl.BlockSpec((None,) + a.shape[1:], lambda b, t: (b,) + (0,) * (a.ndim - 1))
    return pl.pallas_call(
        functools.partial(_nsa_prompt_kernel, n_sb=seq // SEL_BLOCK),
        grid=(batch, nqt),
        in_specs=[pl.BlockSpec((Q_TILE, H_NSA * LANES), lambda b, t: (b * nqt + t, 0)),
                  pl.BlockSpec((Q_TILE, LANES), lambda b, t: (b * nqt + t, 0)),
                  pl.BlockSpec((None, n_chunk, 4 * KV_W), lambda b, t: (b, 0, 0)),
                  seq_blk(slc_tiles), seq_blk(win_tiles),
                  pl.BlockSpec(m_blk.shape, lambda b, t: (0, 0)),
                  pl.BlockSpec(e_tiles.shape, lambda b, t: (0, 0, 0))],
        out_specs=pl.BlockSpec((Q_TILE, NSA_QW), lambda b, t: (b * nqt + t, 0)),
        out_shape=jax.ShapeDtypeStruct((batch * seq, NSA_QW), BF16),
        scratch_shapes=[pltpu.VMEM((G_NSA, HPG * Q_TILE, LANES + m_blk.shape[1]), BF16),
                        pltpu.VMEM((G_NSA, HPG * Q_TILE, LANES), BF16),
                        pltpu.VMEM((G_NSA, HPG * Q_TILE, LANES), F32),
                        pltpu.VMEM((2 * G_NSA, HPG * Q_TILE, LANES), F32),
                        pltpu.VMEM((2 * G_NSA, HPG * Q_TILE, LANES), F32)],
        compiler_params=_cparams(2),
        name="nsa_prompt",
    )(q_n_pad, gate, ht, slc_tiles, win_tiles, m_blk, e_tiles)


def _new_key_tiles(x):
    padded = jnp.concatenate([x, jnp.zeros((KEY_TILE - Q_PAD, x.shape[1]), F32)], axis=0)
    cols = [padded[:, c * LANES:(c + 1) * LANES].T for c in range(x.shape[1] // LANES)]
    return jnp.concatenate(cols, axis=0).astype(BF16)


def _nsa_sample_local_kernel(q_ref, gate_ref, ht_ref, win_ref, neww_ref, mblk_ref,
                             part_ref, sel_ref, *, past, n_sb):
    nq = Q_PAD
    rows = HPG * nq
    qpos = past + lax.broadcasted_iota(jnp.int32, (nq, 1), 0)
    qpos_rows = _rep_rows(qpos, HPG)
    kcol = lax.broadcasted_iota(jnp.int32, (1, KEY_TILE), 1)
    gate = gate_ref[...]
    cmpk, cmpv = _cmp_tables(ht_ref[...])
    n_w = win_ref.shape[1]
    qs, o_c = [], []
    for g in range(G_NSA):
        heads = [g * HPG + h for h in range(HPG)]
        qs.append(_query_rows([q_ref[h * nq:(h + 1) * nq, :] for h in heads], heads))
        p, oc = _cmp_branch(qs[g], cmpk[g], cmpv, qpos_rows)
        o_c.append(oc)
        imp = p[0:nq] + p[nq:2 * nq] + p[2 * nq:3 * nq] + p[3 * nq:4 * nq]
        sel_ref[g * nq:(g + 1) * nq, :] = _select_blocks(imp, mblk_ref[...], qpos, n_sb,
                                                         blocks_on_sublanes=False)
    tiles = [(win_ref[:, t * KEY_TILE:(t + 1) * KEY_TILE].astype(BF16), (past - n_w + t * KEY_TILE) + kcol)
             for t in range(n_w // KEY_TILE)]
    tiles.append((_new_key_tiles(neww_ref[...]), past + kcol))
    carry = [_softmax_init(rows) for _ in range(G_NSA)]
    for tile, kpos in tiles:
        dist = qpos_rows - kpos
        mask = jnp.logical_and(jnp.logical_and(dist >= 0, dist < WINDOW), kpos >= 0)
        pos = _pos_rows(kpos)
        carry = _softmax_tiles([(qs[g], _key_tile(tile[:KV_W], g, pos), _value_tile(tile[KV_W:], g), carry[g])
                                for g in range(G_NSA)], mask)
    for g in range(G_NSA):
        part_ref[g * rows:(g + 1) * rows, :] = (_gate_rows(gate, g, 0, nq) * o_c[g]
                                                + _gate_rows(gate, g, 2, nq) * _softmax_finish(carry[g][1], g))


def _nsa_sample_local(q_n_rows, gate8, ht, win_state_t, new_win8, m_blk, past, n_sb):
    n_seq = q_n_rows.shape[0]
    blk = lambda a: pl.BlockSpec((None,) + a.shape[1:], lambda b: (b,) + (0,) * (a.ndim - 1))
    nbp = m_blk.shape[1]
    return pl.pallas_call(
        functools.partial(_nsa_sample_local_kernel, past=past, n_sb=n_sb),
        grid=(n_seq,),
        in_specs=[blk(q_n_rows), blk(gate8), blk(ht), blk(win_state_t), blk(new_win8),
                  pl.BlockSpec(m_blk.shape, lambda b: (0, 0))],
        out_specs=[pl.BlockSpec((None, H_NSA * Q_PAD, LANES), lambda b: (b, 0, 0)),
                   pl.BlockSpec((None, G_NSA * Q_PAD, nbp), lambda b: (b, 0, 0))],
        out_shape=[jax.ShapeDtypeStruct((n_seq, H_NSA * Q_PAD, LANES), F32),
                   jax.ShapeDtypeStruct((n_seq, G_NSA * Q_PAD, nbp), F32)],
        compiler_params=_cparams(1),
        name="nsa_sample_local",
    )(q_n_rows, gate8, ht, win_state_t, new_win8, m_blk)


def _sb_sample_kernel(*refs, per_step, top_page, past, first_phase):
    live_ref = refs[1]
    pages = refs[2:2 + per_step]
    qsb_ref, newsb_ref, rin_ref, accin_ref, uu_ref, r_ref, acc_ref, on_ref = refs[2 + per_step:]
    b = pl.program_id(0)
    j = pl.program_id(1)
    nq = Q_PAD
    n_pair = H_SB // 2
    qpos2 = _rep_rows(past + lax.broadcasted_iota(jnp.int32, (nq, 1), 0), 2)
    kcol = lax.broadcasted_iota(jnp.int32, (1, KEY_TILE), 1)

    def visit(tiles):
        jobs = [(qsb_ref[pair * 2 * nq:(pair + 1) * 2 * nq, :], qpos2, (r_ref[pair], acc_ref[pair]),
                 [(tile[pair * LANES:(pair + 1) * LANES], tile[SB_W + pair * LANES:SB_W + (pair + 1) * LANES],
                   kpos) for tile, kpos in tiles]) for pair in range(n_pair)]
        r_max = None
        for pair, (r, acc) in enumerate(_sb_tiles(jobs, uu_ref[...])):
            r_ref[pair] = r
            acc_ref[pair] = acc
            r_max = r if r_max is None else jnp.maximum(r_max, r)
        on_ref[0] = (jnp.max(r_max) > SB_EXIT).astype(jnp.int32)

    @pl.when(j == 0)
    def _():
        r_ref[...] = rin_ref[...]
        acc_ref[...] = accin_ref[...]
        if first_phase:
            visit([(_new_key_tiles(newsb_ref[...]), past + kcol)])
        else:
            on_ref[0] = live_ref[b]

    group = 2
    for k0 in range(0, per_step, group):
        @pl.when(on_ref[0] != 0)
        def _(k0=k0):
            tiles = []
            for k in range(k0, min(k0 + group, per_step)):
                pg = top_page - (j * per_step + k)
                tiles.append((pages[k][...].astype(BF16), pg * KEY_TILE + kcol))
            visit(tiles)


def _sb_sample_phase(pages_flat, live, cache_sb_t, q_sb_rows, new_sb8, r_in, acc_in, uu, *, n_pages,
                     top_page, n_visit, per_step, past, first_phase):
    n_seq = q_sb_rows.shape[0]
    n_steps = n_visit // per_step

    def page_spec(k):
        def index(b, j, pt, lv):
            own = pt[b * n_pages + (top_page - (j * per_step + k))]
            return (jnp.where(lv[b] != 0, own, pt[0]), 0, 0)
        return pl.BlockSpec((None, 2 * SB_W, KEY_TILE), index)

    seq_blk = lambda a: pl.BlockSpec((None,) + a.shape[1:], lambda b, j, pt, lv: (b,) + (0,) * (a.ndim - 1))
    grid_spec = pltpu.PrefetchScalarGridSpec(
        num_scalar_prefetch=2,
        grid=(n_seq, n_steps),
        in_specs=[page_spec(k) for k in range(per_step)]
        + [seq_blk(q_sb_rows), seq_blk(new_sb8), seq_blk(r_in), seq_blk(acc_in),
           pl.BlockSpec(uu.shape, lambda b, j, pt, lv: (0, 0))],
        out_specs=[seq_blk(r_in), seq_blk(acc_in)],
        scratch_shapes=[pltpu.SMEM((1,), jnp.int32)],
    )
    return pl.pallas_call(
        functools.partial(_sb_sample_kernel, per_step=per_step, top_page=top_page, past=past,
                          first_phase=first_phase),
        grid_spec=grid_spec,
        out_shape=[jax.ShapeDtypeStruct(r_in.shape, F32), jax.ShapeDtypeStruct(acc_in.shape, F32)],
        compiler_params=_cparams(2),
        name="sb_sample_first" if first_phase else "sb_sample_rest",
    )(pages_flat, live, *([cache_sb_t] * per_step), q_sb_rows, new_sb8, r_in, acc_in, uu)


def _largest_divisor(n, cap):
    return max(d for d in range(1, cap + 1) if n % d == 0)


def _sb_sample(pages_flat, cache_sb_t, q_sb_rows, new_sb8, uu, n_pages, past):
    n_seq = q_sb_rows.shape[0]
    n_pair = H_SB // 2
    n_first = _largest_divisor(n_pages, 4)
    zeros_r = jnp.zeros((n_seq, n_pair, 2 * Q_PAD, LANES), F32)
    zeros_acc = jnp.zeros((n_seq, n_pair, 2 * Q_PAD, LANES), F32)
    everyone = jnp.ones((n_seq,), jnp.int32)
    common = dict(n_pages=n_pages, past=past)
    r, acc = _sb_sample_phase(pages_flat, everyone, cache_sb_t, q_sb_rows, new_sb8, zeros_r, zeros_acc, uu,
                              top_page=n_pages - 1, n_visit=n_first, per_step=n_first, first_phase=True, **common)
    n_rest = n_pages - n_first
    if n_rest:
        live = (jnp.max(r, axis=(1, 2, 3)) > SB_EXIT).astype(jnp.int32)
        r, acc = _sb_sample_phase(pages_flat, live, cache_sb_t, q_sb_rows, new_sb8, r, acc, uu,
                                  top_page=n_pages - 1 - n_first, n_visit=n_rest,
                                  per_step=_largest_divisor(n_rest, 30), first_phase=False, **common)
    low = jnp.arange(LANES) < HEAD_DIM
    o = jnp.where(low, acc[:, :, :Q_PAD], acc[:, :, Q_PAD:])
    return o.transpose(0, 2, 1, 3).reshape(n_seq, Q_PAD, SB_W).astype(BF16)


def _slc_sample_kernel(*refs, per_step, n_pages, past):
    active_ref = refs[1]
    nsa_pages = refs[2:2 + per_step]
    (qn_ref, newslc_ref, sel_ref, part_ref, gate_ref, e_ref, onsa_ref, m_ref, acc_ref) = refs[2 + per_step:]
    b = pl.program_id(0)
    j = pl.program_id(1)
    nq = Q_PAD
    rows = HPG * nq
    qpos4 = _rep_rows(past + lax.broadcasted_iota(jnp.int32, (nq, 1), 0), HPG)
    kcol = lax.broadcasted_iota(jnp.int32, (1, KEY_TILE), 1)

    def visit(tiles, mask):
        wide = lambda parts: parts[0] if len(parts) == 1 else jnp.concatenate(parts, axis=1)
        kt = wide([t[0][:KV_W] for t in tiles])
        vt = wide([t[0][KV_W:] for t in tiles])
        pos = wide([_pos_rows(t[2]) for t in tiles])
        sel_rows = wide([t[1] for t in tiles])
        jobs = []
        for g in range(G_NSA):
            heads = [g * HPG + h for h in range(HPG)]
            not_sel = 1.0 - sel_ref[g * nq:(g + 1) * nq, :]
            q = _query_rows([qn_ref[h * nq:(h + 1) * nq, :] for h in heads], heads, extra=not_sel)
            jobs.append((q, _key_tile(kt, g, pos, sel_rows), _value_tile(vt, g), (m_ref[g], acc_ref[g])))
        for g, (m, acc) in enumerate(_softmax_tiles(jobs, mask)):
            m_ref[g] = m
            acc_ref[g] = acc

    @pl.when(j == 0)
    def _():
        m_ref[...] = jnp.full(m_ref.shape, NEG, F32)
        acc_ref[...] = jnp.zeros(acc_ref.shape, F32)
        visit([(_new_key_tiles(newslc_ref[...]), e_ref[n_pages], past + kcol)], (past + kcol) <= qpos4)

    @pl.when(active_ref[b * pl.num_programs(1) + j] != 0)
    def _():
        tiles = []
        for k in range(per_step):
            pg = n_pages - 1 - (j * per_step + k)
            tiles.append((nsa_pages[k][...].astype(BF16), e_ref[pg], pg * KEY_TILE + kcol))
        visit(tiles, None)

    @pl.when(j == pl.num_programs(1) - 1)
    def _():
        low = _lane_lt((nq, LANES), HEAD_DIM)
        gate = gate_ref[...]
        outs = [part_ref[g * rows:(g + 1) * rows, :] + _gate_rows(gate, g, 1, nq) * _softmax_finish(acc_ref[g], g)
                for g in range(G_NSA)]
        for h in range(HPG):
            o = jnp.where(low, outs[0][h * nq:(h + 1) * nq], outs[1][h * nq:(h + 1) * nq])
            onsa_ref[:, h * LANES:(h + 1) * LANES] = o.astype(BF16)


def _slc_sample(pages_flat, need, cache_nsa_t, q_n_rows, new_slc8, sel, part, gate8, e_tiles, n_pages, past):
    n_seq = q_n_rows.shape[0]
    per_step = _largest_divisor(n_pages, 16)
    n_steps = n_pages // per_step
    by_step = need[:, ::-1].reshape(n_seq, n_steps, per_step)
    step = jnp.arange(n_steps, dtype=jnp.int32)[None, :, None]
    shown_step = jnp.maximum(lax.cummax(jnp.where(by_step != 0, step, -1), axis=1), 0)
    slot = jnp.arange(per_step, dtype=jnp.int32)[None, None, :]
    shown_page = n_pages - 1 - (shown_step * per_step + slot)
    fetch = jnp.take_along_axis(pages_flat.reshape(n_seq, n_pages), shown_page.reshape(n_seq, -1), axis=1)
    fetch_flat = fetch.reshape(-1).astype(jnp.int32)
    active_flat = jnp.max(by_step, axis=2).reshape(-1).astype(jnp.int32)

    def nsa_spec(k):
        return pl.BlockSpec((None, 2 * KV_W, KEY_TILE),
                            lambda b, j, ft, nd: (ft[(b * n_steps + j) * per_step + k], 1, 0))

    seq_blk = lambda a: pl.BlockSpec((None,) + a.shape[1:], lambda b, j, ft, nd: (b,) + (0,) * (a.ndim - 1))
    grid_spec = pltpu.PrefetchScalarGridSpec(
        num_scalar_prefetch=2,
        grid=(n_seq, n_steps),
        in_specs=[nsa_spec(k) for k in range(per_step)]
        + [seq_blk(q_n_rows), seq_blk(new_slc8), seq_blk(sel), seq_blk(part), seq_blk(gate8),
           pl.BlockSpec(e_tiles.shape, lambda b, j, ft, nd: (0, 0, 0))],
        out_specs=pl.BlockSpec((None, Q_PAD, NSA_QW), lambda b, j, ft, nd: (b, 0, 0)),
        scratch_shapes=[pltpu.VMEM((G_NSA, HPG * Q_PAD, LANES), F32),
                        pltpu.VMEM((G_NSA, HPG * Q_PAD, LANES), F32)],
    )
    return pl.pallas_call(
        functools.partial(_slc_sample_kernel, per_step=per_step, n_pages=n_pages, past=past),
        grid_spec=grid_spec,
        out_shape=jax.ShapeDtypeStruct((n_seq, Q_PAD, NSA_QW), BF16),
        compiler_params=_cparams(2),
        name="slc_sample",
    )(fetch_flat, active_flat, *([cache_nsa_t] * per_step), q_n_rows, new_slc8, sel, part, gate8, e_tiles)


def _block_sum_matrix(n_chunk_pad, n_c, n_sb, nbp):
    per = SEL_BLOCK // CMP_STRIDE
    c = jnp.arange(n_chunk_pad)[:, None]
    b = jnp.arange(nbp)[None, :]
    own = (c // per == b)
    nxt = jnp.logical_and((c + 1) // per == b, c + 1 < n_sb * per)
    m = (own.astype(F32) + nxt.astype(F32)) * (c < n_c) * (b < n_sb)
    return m.astype(BF16)


def _expand_tiles(n_tiles, nbp):
    t = jnp.arange(n_tiles)[:, None, None]
    b = jnp.arange(nbp)[None, :, None]
    k = jnp.arange(KEY_TILE)[None, None, :]
    return (((t * KEY_TILE + k) // SEL_BLOCK == b) * MASKED).astype(BF16)


def _suffix_matrix():
    j = jnp.arange(KEY_TILE)[:, None]
    s = jnp.arange(KEY_TILE)[None, :]
    u = (j > s).astype(BF16)
    return jnp.concatenate([u, u], axis=0)


def _rows_per_seq(x, n_seq, n_new, slots):
    x = x.reshape(n_seq, n_new, slots, LANES)
    x = jnp.pad(x, ((0, 0), (0, Q_PAD - n_new), (0, 0), (0, 0)))
    return x.transpose(0, 2, 1, 3).reshape(n_seq, slots * Q_PAD, LANES)


def _pad_new(x, n_seq, n_new):
    x = x.reshape(n_seq, n_new, x.shape[-1])
    return jnp.pad(x, ((0, 0), (0, Q_PAD - n_new), (0, 0)))


def _token_minor(x, lead):
    n = x.ndim
    perm = tuple(range(lead)) + tuple(range(lead + 1, n)) + (lead,)
    x = x.transpose(perm)
    return x.reshape(x.shape[:lead] + (-1, x.shape[-1]))


def _token_major(x_t, lead, feat_shape):
    x = x_t.reshape(x_t.shape[:lead] + tuple(feat_shape) + (x_t.shape[-1],))
    n = x.ndim
    perm = tuple(range(lead)) + (n - 1,) + tuple(range(lead, n - 1))
    return x.transpose(perm)


def kernel(x_prompt, x_sample, cache_sb_kv, cache_nsa_kv, state_win_kv, page_table, c_prompt, c_sample,
           w_ada, b_ada, w_ff1_up, w_ff1_down, w_in, w_cmp_k, w_cmp_v, w_o, w_ff2_up, w_ff2_down,
           ln_g, ln_b):
    batch, seq, d = x_prompt.shape
    n_seq, n_new, _ = x_sample.shape
    depth, n_pool, page = cache_sb_kv.shape[:3]
    n_pages = page_table.shape[1]
    past = n_pages * page
    n_w = state_win_kv.shape[2]
    alpha = (2.0 * depth) ** 0.25
    assert page == KEY_TILE and seq % TOKEN_TILE == 0 and n_new <= Q_PAD and n_w % KEY_TILE == 0
    assert n_new < CMP_STRIDE and past % SEL_BLOCK == 0

    uu = _suffix_matrix()
    n_chunk_p = seq // CMP_STRIDE
    n_sb_p = seq // SEL_BLOCK
    m_blk_p = _block_sum_matrix(n_chunk_p, n_chunk_p - 1, n_sb_p, LANES * pl.cdiv(n_sb_p, LANES))
    e_p = _expand_tiles(seq // KEY_TILE, m_blk_p.shape[1])
    t_all = past + n_new
    n_sb_s = pl.cdiv(t_all, SEL_BLOCK)
    n_chunk_s = past // CMP_STRIDE
    m_blk_s = _block_sum_matrix(n_chunk_s, t_all // CMP_STRIDE - 1, n_sb_s,
                                LANES * pl.cdiv(n_sb_s, LANES))
    e_s = _expand_tiles(n_pages + 1, m_blk_s.shape[1])
    blocks_per_page = KEY_TILE // SEL_BLOCK

    xp = x_prompt.reshape(batch * seq, d)
    xs = x_sample.reshape(n_seq * n_new, d)
    c_all = jnp.concatenate([c_prompt, c_sample], axis=0)
    cache_sb_t = _token_minor(cache_sb_kv, 2).reshape(depth * n_pool, 2 * SB_W, page)
    cache_nsa_t = _token_minor(cache_nsa_kv, 2).reshape(depth * n_pool, 4 * KV_W, page)
    win_state_t = _token_minor(state_win_kv, 2)
    tiles_per_seq = seq // TOKEN_TILE
    states = [[] for _ in range(6)]

    for layer in range(depth):
        mod = _ada(c_all, w_ada[layer], b_ada[layer]).reshape(batch + n_seq, N_MOD, d)
        mod_p = mod[:batch].transpose(1, 0, 2)[:, :, None, :]
        mod_s = jnp.repeat(mod[batch:], n_new, axis=0).transpose(1, 0, 2)[:, None, :, :]
        w_up1, w_dn1 = w_ff1_up[layer].astype(BF16), w_ff1_down[layer].astype(BF16)
        w_up2, w_dn2 = w_ff2_up[layer].astype(BF16), w_ff2_down[layer].astype(BF16)
        w_proj = _proj_weights(w_in[layer])
        w_kv_t = w_proj[:, _P_KVSB:].T
        w_cmp = _compress_weights(w_cmp_k[layer], w_cmp_v[layer])
        w_osb = w_o[layer, :SB_W].astype(BF16)
        w_onsa = (w_o[layer, SB_W:].reshape(G_NSA, HPG, HEAD_DIM, d).transpose(1, 0, 2, 3)
                  .reshape(NSA_QW, d).astype(BF16))
        lng, lnb = ln_g[layer], ln_b[layer]
        pages_flat = (page_table + layer * n_pool).reshape(-1).astype(jnp.int32)

        kw_p = dict(per_token=False, tiles_per_seq=tiles_per_seq)
        x1 = _ffn(xp, mod_p, w_up1, w_dn1, lng[0:1], lnb[0:1], mod_base=0, alpha=alpha, **kw_p)
        (q_sb, q_n, gate, cmp_rows, kv_sb_t, kv_nsa_t, kv_win_t, sb_tiles, slc_tiles, win_tiles) = _proj_prompt(
            x1, mod_p, w_proj, w_kv_t, batch, seq, mod_base=3)
        ht = _compress_prompt(cmp_rows, w_cmp, batch, seq)
        o_sb = _sb_prompt(q_sb, sb_tiles, uu, batch, seq)
        o_nsa = _nsa_prompt(q_n, gate, ht, slc_tiles, win_tiles, m_blk_p, e_p, batch, seq)
        xp = _ffn(x1, mod_p, w_up2, w_dn2, lng[1:3], lnb[1:3], mod_base=6, alpha=alpha,
                  oproj=(o_sb, o_nsa, w_osb, w_onsa), **kw_p)
        n_keep = min(WINDOW, seq)
        states[0].append(_token_major(kv_sb_t, 1, (2, H_SB, HEAD_DIM)))
        states[1].append(_token_major(kv_nsa_t, 1, (4, G_NSA, HEAD_DIM)))
        states[2].append(_token_major(kv_win_t[:, :, seq - n_keep:], 1, (2, G_NSA, HEAD_DIM)))

        kw_s = dict(per_token=True, tiles_per_seq=1)
        s1 = _ffn(xs, mod_s, w_up1, w_dn1, lng[0:1], lnb[0:1], mod_base=0, alpha=alpha, **kw_s)
        q_sb, q_n, gate, kv_sb, kv_nsa, kv_win = _proj_sample(s1, mod_s, w_proj, mod_base=3)
        ht = _compress_pages(cache_nsa_t, pages_flat, w_cmp, n_seq, n_pages)
        q_n_rows = _rows_per_seq(q_n, n_seq, n_new, H_NSA)
        q_sb_rows = _rows_per_seq(q_sb, n_seq, n_new, H_SB)
        gate8 = _pad_new(gate, n_seq, n_new)
        part, sel = _nsa_sample_local(q_n_rows, gate8, ht, win_state_t[layer],
                                      _pad_new(kv_win, n_seq, n_new), m_blk_s, past, n_sb_s)
        page_sel = sel[:, :, :n_pages * blocks_per_page].reshape(n_seq, -1, n_pages, blocks_per_page)
        need = (jnp.max(page_sel, axis=(1, 3)) > 0.5).astype(jnp.int32)
        o_sb = _sb_sample(pages_flat, cache_sb_t, q_sb_rows, _pad_new(kv_sb, n_seq, n_new), uu, n_pages, past)
        o_nsa = _slc_sample(pages_flat, need, cache_nsa_t, q_n_rows, _pad_new(kv_nsa[:, 2 * KV_W:], n_seq, n_new),
                            sel, part, gate8, e_s, n_pages, past)
        o_sb = o_sb[:, :n_new].reshape(n_seq * n_new, SB_W)
        o_nsa = o_nsa[:, :n_new].reshape(n_seq * n_new, NSA_QW)
        xs = _ffn(s1, mod_s, w_up2, w_dn2, lng[1:3], lnb[1:3], mod_base=6, alpha=alpha,
                  oproj=(o_sb, o_nsa, w_osb, w_onsa), **kw_s)
        new_win_t = jnp.concatenate([win_state_t[layer][:, :, n_new:],
                                     kv_win.reshape(n_seq, n_new, 2 * KV_W).transpose(0, 2, 1)], axis=2)
        states[3].append(kv_sb.reshape(n_seq, n_new, 2, H_SB, HEAD_DIM))
        states[4].append(kv_nsa.reshape(n_seq, n_new, 4, G_NSA, HEAD_DIM))
        states[5].append(_token_major(new_win_t, 1, (2, G_NSA, HEAD_DIM)))

    return (xp.reshape(batch, seq, d), xs.reshape(n_seq, n_new, d)) + tuple(jnp.stack(s) for s in states)
```

```python
import functools

import jax
import jax.numpy as jnp
from jax import lax
from jax.experimental import pallas as pl
from jax.experimental.pallas import tpu as pltpu

F32 = jnp.float32
BF16 = jnp.bfloat16

HEAD_DIM = 64
H_SB = 8
H_NSA = 8
G_NSA = 2
HPG = H_NSA // G_NSA
CMP_BLOCK = 32
CMP_STRIDE = 16
SEL_BLOCK = 64
N_SELECT = 16
WINDOW = 512
N_MOD = 9
LN_EPS = 1e-5
FORCE_BONUS = 1.0e4
NEG = -1.0e30
MASKED = -2.0e30

SB_W = H_SB * HEAD_DIM
NSA_QW = H_NSA * HEAD_DIM
KV_W = G_NSA * HEAD_DIM
OFF_NSA_Q = 3 * SB_W
OFF_NSA_KV = OFF_NSA_Q + NSA_QW
OFF_WIN = OFF_NSA_KV + 4 * KV_W
OFF_GATE = OFF_WIN + 2 * KV_W
N_IN = OFF_GATE + 3 * H_NSA

LANES = 128
KEY_TILE = 128
VMEM_LIMIT = 56 * 1024 * 1024

TOKEN_TILE = 512
Q_TILE = 128
FF_CHUNK = 256
Q_PAD = 8
SB_EXIT = -88.0


def _cparams(n_axes):
    return pltpu.CompilerParams(dimension_semantics=("arbitrary",) * n_axes,
                                vmem_limit_bytes=VMEM_LIMIT)


def _dot(a, b):
    return jnp.dot(a, b, preferred_element_type=F32)


def _dot_nt(a, b):
    return lax.dot_general(a, b, (((1,), (1,)), ((), ())), preferred_element_type=F32)


def _split_bf16(x, parts):
    out = []
    r = x
    for _ in range(parts):
        h = r.astype(BF16)
        out.append(h)
        r = r - h.astype(F32)
    return out


def _sigmoid(x):
    return 1.0 / (1.0 + jnp.exp(-x))


def _layer_norm(y, g, b):
    mu = jnp.mean(y, axis=-1, keepdims=True)
    d = y - mu
    var = jnp.mean(d * d, axis=-1, keepdims=True)
    return d * lax.rsqrt(var + LN_EPS) * g + b


def _ada_kernel(c_ref, w_ref, b_ref, o_ref):
    c = c_ref[...]
    a = c * _sigmoid(c)
    a_hi, a_lo = _split_bf16(a, 2)
    w_hi, w_lo = _split_bf16(w_ref[...], 2)
    o_ref[...] = _dot(a_hi, w_hi) + (_dot(a_hi, w_lo) + _dot(a_lo, w_hi)) + b_ref[...]


def _ada(c, w, b):
    n, d = c.shape
    n_out = w.shape[1]
    bn = 1024
    return pl.pallas_call(
        _ada_kernel,
        grid=(n_out // bn,),
        in_specs=[pl.BlockSpec((n, d), lambda j: (0, 0)),
                  pl.BlockSpec((d, bn), lambda j: (0, j)),
                  pl.BlockSpec((1, bn), lambda j: (0, j))],
        out_specs=pl.BlockSpec((n, bn), lambda j: (0, j)),
        out_shape=jax.ShapeDtypeStruct((n, n_out), F32),
        compiler_params=_cparams(1),
        name="ada",
    )(c, w, b.reshape(1, n_out))


def _mod_spec(per_token, tm, d, tiles_per_seq):
    if per_token:
        return pl.BlockSpec((N_MOD, None, tm, d), lambda i: (0, 0, i, 0))
    return pl.BlockSpec((N_MOD, None, 1, d), lambda i: (0, i // tiles_per_seq, 0, 0))


def _ffn_kernel(*refs, mod_base, d_ff, alpha, fuse_oproj):
    if fuse_oproj:
        (x_ref, osb_ref, onsa_ref, mod_ref, wosb_ref, wonsa_ref, wup_ref, wdn_ref,
         lng_ref, lnb_ref, o_ref, acc_ref) = refs
        mixed = _dot(osb_ref[...], wosb_ref[...]) + _dot(onsa_ref[...], wonsa_ref[...])
        x = _layer_norm(alpha * x_ref[...] + mod_ref[mod_base - 1] * mixed,
                        lng_ref[0:1, :], lnb_ref[0:1, :])
        ln_row = 1
    else:
        x_ref, mod_ref, wup_ref, wdn_ref, lng_ref, lnb_ref, o_ref, acc_ref = refs
        x = x_ref[...]
        ln_row = 0
    u = (x * (1.0 + mod_ref[mod_base + 1]) + mod_ref[mod_base]).astype(BF16)
    for c in range(d_ff // FF_CHUNK):
        lo = c * FF_CHUNK
        g = _dot(u, wup_ref[:, lo:lo + FF_CHUNK])
        v = _dot(u, wup_ref[:, d_ff + lo:d_ff + lo + FF_CHUNK])
        act = (g * _sigmoid(g) * v).astype(BF16)
        part = _dot(act, wdn_ref[lo:lo + FF_CHUNK, :])
        if c == 0:
            acc_ref[...] = part
        else:
            acc_ref[...] += part
    y = alpha * x + 0.5 * mod_ref[mod_base + 2] * acc_ref[...]
    o_ref[...] = _layer_norm(y, lng_ref[ln_row:ln_row + 1, :], lnb_ref[ln_row:ln_row + 1, :])


def _ffn(x, mod4, w_up, w_down, ln_g2, ln_b2, *, mod_base, alpha, per_token, tiles_per_seq,
         oproj=None):
    t, d = x.shape
    d_ff = w_down.shape[0]
    tm = min(TOKEN_TILE, t)
    row = lambda i: (i, 0)
    const = lambda i: (0, 0)
    in_specs = [pl.BlockSpec((tm, d), row)]
    args = [x]
    if oproj is not None:
        o_sb, o_nsa, w_osb, w_onsa = oproj
        in_specs += [pl.BlockSpec((tm, SB_W), row), pl.BlockSpec((tm, NSA_QW), row)]
        args += [o_sb, o_nsa]
    in_specs.append(_mod_spec(per_token, tm, d, tiles_per_seq))
    args.append(mod4)
    if oproj is not None:
        in_specs += [pl.BlockSpec((SB_W, d), const), pl.BlockSpec((NSA_QW, d), const)]
        args += [w_osb, w_onsa]
    in_specs += [pl.BlockSpec((d, 2 * d_ff), const), pl.BlockSpec((d_ff, d), const),
                 pl.BlockSpec(ln_g2.shape, const), pl.BlockSpec(ln_b2.shape, const)]
    args += [w_up, w_down, ln_g2, ln_b2]
    return pl.pallas_call(
        functools.partial(_ffn_kernel, mod_base=mod_base, d_ff=d_ff, alpha=alpha,
                          fuse_oproj=oproj is not None),
        grid=(t // tm,),
        in_specs=in_specs,
        out_specs=pl.BlockSpec((tm, d), row),
        out_shape=jax.ShapeDtypeStruct((t, d), F32),
        scratch_shapes=[pltpu.VMEM((tm, d), F32)],
        compiler_params=_cparams(1),
        name="ffn_oproj" if oproj is not None else "ffn",
    )(*args)


_P_QSB = 0
_P_QN = _P_QSB + H_SB * LANES
_P_GATE = _P_QN + H_NSA * LANES
_P_KVSB = _P_GATE + LANES
_P_KVN = _P_KVSB + 2 * SB_W
_P_WIN = _P_KVN + 4 * KV_W
_P_END = _P_WIN + 2 * KV_W
_KV_ROWS = _P_END - _P_KVSB


def _proj_weights(w_in):
    d = w_in.shape[0]
    scale = HEAD_DIM ** -0.5
    q_sb = (w_in[:, :SB_W] * scale).reshape(d, H_SB, HEAD_DIM)
    q_sb_pad = jnp.zeros((d, H_SB, LANES), F32)
    for h in range(H_SB):
        o = (h % 2) * HEAD_DIM
        q_sb_pad = q_sb_pad.at[:, h, o:o + HEAD_DIM].set(q_sb[:, h])
    q_n = (w_in[:, OFF_NSA_Q:OFF_NSA_KV] * scale).reshape(d, H_NSA, HEAD_DIM)
    q_n_pad = jnp.zeros((d, H_NSA, LANES), F32).at[:, :, :HEAD_DIM].set(q_n)
    gate = jnp.pad(w_in[:, OFF_GATE:], ((0, 0), (0, LANES - 3 * H_NSA)))
    w = jnp.concatenate([q_sb_pad.reshape(d, -1), q_n_pad.reshape(d, -1), gate, w_in[:, SB_W:3 * SB_W],
                         w_in[:, OFF_NSA_KV:OFF_WIN], w_in[:, OFF_WIN:OFF_GATE]], axis=1)
    return w.astype(BF16)


def _proj_rows(u, w_ref, qsb_ref, qn_ref, gate_ref):
    qsb_ref[...] = _dot(u, w_ref[:, _P_QSB:_P_QN]).astype(BF16)
    qn_ref[...] = _dot(u, w_ref[:, _P_QN:_P_GATE]).astype(BF16)
    gate_ref[...] = _sigmoid(_dot(u, w_ref[:, _P_GATE:_P_KVSB]))


def _proj_sample_kernel(x_ref, mod_ref, w_ref, qsb_ref, qn_ref, gate_ref, kvsb_ref, kvn_ref, win_ref,
                        *, mod_base):
    u = (x_ref[...] * (1.0 + mod_ref[mod_base + 1]) + mod_ref[mod_base]).astype(BF16)
    _proj_rows(u, w_ref, qsb_ref, qn_ref, gate_ref)
    kvsb_ref[...] = _dot(u, w_ref[:, _P_KVSB:_P_KVN])
    kvn_ref[...] = _dot(u, w_ref[:, _P_KVN:_P_WIN])
    win_ref[...] = _dot(u, w_ref[:, _P_WIN:_P_END])


def _proj_sample(x, mod4, w_proj, *, mod_base):
    t, d = x.shape
    row = lambda i: (i, 0)
    widths = [(H_SB * LANES, BF16), (H_NSA * LANES, BF16), (LANES, F32), (2 * SB_W, F32),
              (4 * KV_W, F32), (2 * KV_W, F32)]
    return pl.pallas_call(
        functools.partial(_proj_sample_kernel, mod_base=mod_base),
        grid=(1,),
        in_specs=[pl.BlockSpec((t, d), row), _mod_spec(True, t, d, 1),
                  pl.BlockSpec(w_proj.shape, lambda i: (0, 0))],
        out_specs=[pl.BlockSpec((t, w), row) for w, _ in widths],
        out_shape=[jax.ShapeDtypeStruct((t, w), dt) for w, dt in widths],
        compiler_params=_cparams(1),
        name="proj_sample",
    )(x, mod4, w_proj)


def _proj_prompt_kernel(x_ref, mod_ref, w_ref, wt_ref, qsb_ref, qn_ref, gate_ref, cmp_ref,
                        sbt_ref, nsat_ref, wint_ref, sbtile_ref, slctile_ref, wintile_ref, *, mod_base):
    u = (x_ref[...] * (1.0 + mod_ref[mod_base + 1]) + mod_ref[mod_base]).astype(BF16)
    _proj_rows(u, w_ref, qsb_ref, qn_ref, gate_ref)
    cmp_ref[...] = _dot(u, w_ref[:, _P_KVN:_P_KVN + 2 * KV_W])
    kvt = _dot_nt(wt_ref[...], u)
    n_sb, n_nsa = 2 * SB_W, 4 * KV_W
    sbt_ref[...] = kvt[:n_sb]
    nsat_ref[...] = kvt[n_sb:n_sb + n_nsa]
    wint_ref[...] = kvt[n_sb + n_nsa:]
    kvb = kvt.astype(BF16)
    for j in range(sbtile_ref.shape[0]):
        cols = slice(j * KEY_TILE, (j + 1) * KEY_TILE)
        sbtile_ref[j] = kvb[:n_sb, cols]
        slctile_ref[j] = kvb[n_sb + 2 * KV_W:n_sb + n_nsa, cols]
        wintile_ref[j] = kvb[n_sb + n_nsa:, cols]


def _proj_prompt(x, mod4, w_proj, w_kv_t, batch, seq, *, mod_base):
    t, d = x.shape
    tm = TOKEN_TILE
    tps = seq // tm
    kt = tm // KEY_TILE
    nkt = seq // KEY_TILE
    row = lambda i: (i, 0)
    tr = lambda i: (i // tps, 0, i % tps)
    tile = lambda i: (i // tps, i % tps, 0, 0)
    rows = [(H_SB * LANES, BF16), (H_NSA * LANES, BF16), (LANES, F32), (2 * KV_W, F32)]
    trans = [2 * SB_W, 4 * KV_W, 2 * KV_W]
    tiles = [2 * SB_W, 2 * KV_W, 2 * KV_W]
    return pl.pallas_call(
        functools.partial(_proj_prompt_kernel, mod_base=mod_base),
        grid=(t // tm,),
        in_specs=[pl.BlockSpec((tm, d), row), _mod_spec(False, tm, d, tps),
                  pl.BlockSpec(w_proj.shape, lambda i: (0, 0)),
                  pl.BlockSpec(w_kv_t.shape, lambda i: (0, 0))],
        out_specs=[pl.BlockSpec((tm, w), row) for w, _ in rows]
        + [pl.BlockSpec((None, r, tm), tr) for r in trans]
        + [pl.BlockSpec((None, kt, r, KEY_TILE), tile) for r in tiles],
        out_shape=[jax.ShapeDtypeStruct((t, w), dt) for w, dt in rows]
        + [jax.ShapeDtypeStruct((batch, r, seq), F32) for r in trans]
        + [jax.ShapeDtypeStruct((batch, nkt, r, KEY_TILE), BF16) for r in tiles],
        compiler_params=_cparams(1),
        name="proj_prompt",
    )(x, mod4, w_proj, w_kv_t)


def _compress_weights(w_cmp_k, w_cmp_v):
    n_half = CMP_BLOCK // CMP_STRIDE
    kinds = []
    for w in (w_cmp_k, w_cmp_v):
        w = w.reshape(n_half, CMP_STRIDE, HEAD_DIM, HEAD_DIM)
        blk = jnp.zeros((n_half, CMP_STRIDE, KV_W, KV_W), F32)
        for g in range(G_NSA):
            lo = g * HEAD_DIM
            blk = blk.at[:, :, lo:lo + HEAD_DIM, lo:lo + HEAD_DIM].set(w)
        kinds.append(jnp.concatenate([blk[0], blk[1]], axis=-1).reshape(CMP_STRIDE * KV_W, 2 * KV_W))
    return jnp.stack(kinds, axis=0).astype(BF16)


def _compress_rows(xk_ref, xv_ref, w_ref, o_ref):
    n_chunk = o_ref.shape[0]
    for kind, x_ref in enumerate((xk_ref, xv_ref)):
        chunk_rows = jnp.concatenate([x_ref[pl.ds(i, n_chunk, stride=CMP_STRIDE), :].astype(BF16)
                                      for i in range(CMP_STRIDE)], axis=1)
        acc = _dot(chunk_rows, w_ref[kind])
        o_ref[:, kind * KV_W:(kind + 1) * KV_W] = acc[:, :KV_W]
        o_ref[:, (2 + kind) * KV_W:(3 + kind) * KV_W] = acc[:, KV_W:]


def _compress_prompt_kernel(xk_ref, xv_ref, w_ref, o_ref):
    _compress_rows(xk_ref, xv_ref, w_ref, o_ref)


def _compress_pages_kernel(*refs, per_step):
    page_refs = refs[1:1 + per_step]
    w_ref, o_ref, xk_ref, xv_ref = refs[1 + per_step:]
    eye = jnp.where(lax.broadcasted_iota(jnp.int32, (KEY_TILE, KEY_TILE), 0)
                    == lax.broadcasted_iota(jnp.int32, (KEY_TILE, KEY_TILE), 1), 1.0, 0.0).astype(BF16)
    for k, p_ref in enumerate(page_refs):
        rows = slice(k * KEY_TILE, (k + 1) * KEY_TILE)
        xk_ref[rows, :] = _dot_nt(eye, p_ref[:KV_W, :].astype(BF16))
        xv_ref[rows, :] = _dot_nt(eye, p_ref[KV_W:, :].astype(BF16))
    _compress_rows(xk_ref, xv_ref, w_ref, o_ref)


def _compress_prompt(kv_nsa, w_cmp, batch, seq):
    n_chunk = seq // CMP_STRIDE
    return pl.pallas_call(
        _compress_prompt_kernel,
        grid=(batch,),
        in_specs=[pl.BlockSpec((seq, KV_W), lambda b: (b, 0)),
                  pl.BlockSpec((seq, KV_W), lambda b: (b, 1)),
                  pl.BlockSpec(w_cmp.shape, lambda b: (0, 0, 0))],
        out_specs=pl.BlockSpec((None, n_chunk, 4 * KV_W), lambda b: (b, 0, 0)),
        out_shape=jax.ShapeDtypeStruct((batch, n_chunk, 4 * KV_W), F32),
        compiler_params=_cparams(1),
        name="compress_prompt",
    )(kv_nsa, kv_nsa, w_cmp)


def _compress_pages(cache_nsa_t, pages_flat, w_cmp, n_seq, n_pages):
    per_step = min(32, n_pages)
    n_steps = n_pages // per_step
    chunks = per_step * KEY_TILE // CMP_STRIDE

    def page_spec(k):
        return pl.BlockSpec((None, 2 * KV_W, KEY_TILE),
                            lambda b, j, pt: (pt[b * n_pages + j * per_step + k], 0, 0))

    grid_spec = pltpu.PrefetchScalarGridSpec(
        num_scalar_prefetch=1,
        grid=(n_seq, n_steps),
        in_specs=[page_spec(k) for k in range(per_step)]
        + [pl.BlockSpec(w_cmp.shape, lambda b, j, pt: (0, 0, 0))],
        out_specs=pl.BlockSpec((None, chunks, 4 * KV_W), lambda b, j, pt: (b, j, 0)),
        scratch_shapes=[pltpu.VMEM((per_step * KEY_TILE, KV_W), F32),
                        pltpu.VMEM((per_step * KEY_TILE, KV_W), F32)],
    )
    return pl.pallas_call(
        functools.partial(_compress_pages_kernel, per_step=per_step),
        grid_spec=grid_spec,
        out_shape=jax.ShapeDtypeStruct((n_seq, n_steps * chunks, 4 * KV_W), F32),
        compiler_params=_cparams(2),
        name="compress_pages",
    )(pages_flat, *([cache_nsa_t] * per_step), w_cmp)


def _rep_rows(x, n):
    return jnp.concatenate([x] * n, axis=0)


def _masked_softmax(s, mask):
    s = jnp.where(mask, s, NEG)
    m = jnp.max(s, axis=-1, keepdims=True)
    e = jnp.where(mask, jnp.exp(s - m), 0.0)
    return e / jnp.maximum(jnp.sum(e, axis=-1, keepdims=True), 1e-30)


def _with_positions(x, pos):
    lane = lax.broadcasted_iota(jnp.int32, x.shape, x.ndim - 1)
    hi = jnp.right_shift(pos, 8).astype(F32)
    lo = jnp.bitwise_and(pos, 255).astype(F32)
    return jnp.where(lane == HEAD_DIM, hi, jnp.where(lane == HEAD_DIM + 1, lo, x))


def _query_rows(q_slots, heads, extra=None):
    rows = []
    for q, h in zip(q_slots, heads):
        slope = 2.0 ** -(h + 1)
        lane = lax.broadcasted_iota(jnp.int32, q.shape, 1)
        q = jnp.where(lane == HEAD_DIM, 256.0 * slope, jnp.where(lane == HEAD_DIM + 1, slope, q.astype(F32)))
        rows.append(q if extra is None else jnp.concatenate([q, extra], axis=1))
    return jnp.concatenate(rows, axis=0).astype(BF16)


def _cmp_tables(ht):
    n_chunk = ht.shape[0]
    cmp = ht[:, :2 * KV_W] + pltpu.roll(ht[:, 2 * KV_W:], n_chunk - 1, 0)
    k = cmp[:, :KV_W]
    low = _lane_lt(k.shape, HEAD_DIM)
    cend = lax.broadcasted_iota(jnp.int32, (n_chunk, 1), 0) * CMP_STRIDE + (CMP_BLOCK - 1)
    keys = [_with_positions(jnp.where(low, k if g == 0 else pltpu.roll(k, HEAD_DIM, 1), 0.0), cend).astype(BF16)
            for g in range(G_NSA)]
    return keys, cmp[:, KV_W:].astype(BF16)


def _cmp_branch(qs, cmpk, cmpv, qpos_rows):
    n_chunk = cmpk.shape[0]
    s = _dot_nt(qs, cmpk)
    cidx = lax.broadcasted_iota(jnp.int32, (1, n_chunk), 1)
    cend = cidx * CMP_STRIDE + (CMP_BLOCK - 1)
    mask = jnp.logical_and(cend <= qpos_rows, cidx < n_chunk - 1)
    p = _masked_softmax(s, mask)
    return p, _dot(p.astype(BF16), cmpv)


def _select_blocks(imp, m_blk, qpos_col, n_sb, *, blocks_on_sublanes):
    nbp = m_blk.shape[1]
    imp_blk = sum(_dot(part, m_blk) for part in _split_bf16(imp, 3))
    bidx = lax.broadcasted_iota(jnp.int32, (1, nbp), 1)
    tb = jnp.right_shift(qpos_col, SEL_BLOCK.bit_length() - 1)
    valid = jnp.logical_and(bidx * SEL_BLOCK <= qpos_col, bidx < n_sb)
    forced = jnp.logical_or(bidx == 0, jnp.logical_or(bidx == tb, bidx == tb - 1))
    score = jnp.where(valid, imp_blk + jnp.where(forced, FORCE_BONUS, 0.0), -jnp.inf)
    k_sel = min(N_SELECT, n_sb)
    if blocks_on_sublanes:
        n_rows = 8 * pl.cdiv(n_sb, 8)
        st = score.T[:n_rows]
        ridx = lax.broadcasted_iota(jnp.int32, (n_rows, 1), 0)
        cnt = jnp.zeros(st.shape, F32)
        for bp in range(n_sb):
            row = st[bp:bp + 1, :]
            beats = jnp.logical_or(row > st, jnp.logical_and(row == st, ridx > bp))
            cnt = cnt + jnp.where(beats, 1.0, 0.0)
        few = jnp.where(cnt < k_sel, 1.0, 0.0)
        if n_rows < nbp:
            few = jnp.concatenate([few, jnp.zeros((nbp - n_rows, few.shape[1]), F32)], axis=0)
        few = few.T
    else:
        cnt = jnp.zeros(score.shape, F32)
        for bp in range(n_sb):
            col = score[:, bp:bp + 1]
            beats = jnp.logical_or(col > score, jnp.logical_and(col == score, bidx > bp))
            cnt = cnt + jnp.where(beats, 1.0, 0.0)
        few = jnp.where(cnt < k_sel, 1.0, 0.0)
    return jnp.where(valid, few, 0.0)


def _softmax_tiles(jobs, mask):
    logits = [_dot(j[0], j[1]) for j in jobs]
    mids = []
    for s, (_, _, _, (m, _)) in zip(logits, jobs):
        if mask is not None:
            w = mask.shape[1]
            head = jnp.where(mask, s[:, :w], MASKED)
            s = head if w == s.shape[1] else jnp.concatenate([head, s[:, w:]], axis=1)
        m_new = jnp.maximum(m, jnp.max(s, axis=-1, keepdims=True))
        m_wide = jnp.concatenate([m_new] * (s.shape[1] // LANES), axis=1)
        mids.append((m_new, jnp.exp(m - m_new), jnp.exp(s - m_wide).astype(BF16)))
    return [(m_new, a * j[3][1] + _dot_nt(p, j[2])) for (m_new, a, p), j in zip(mids, jobs)]


def _value_tile(vt, g):
    ones = jnp.ones((HEAD_DIM, vt.shape[1]), BF16)
    return jnp.concatenate([vt[:HEAD_DIM], ones] if g == 0 else [ones, vt[HEAD_DIM:]], axis=0)


def _key_tile(kt, g, pos_rows, mask_rows=None):
    parts = [kt[g * HEAD_DIM:(g + 1) * HEAD_DIM], pos_rows]
    if mask_rows is not None:
        parts.append(mask_rows)
    return jnp.concatenate(parts, axis=0)


def _softmax_init(rows):
    return (jnp.full((rows, LANES), NEG, F32), jnp.zeros((rows, LANES), F32))


def _softmax_finish(acc, g):
    lane = (1 - g) * HEAD_DIM
    return acc / jnp.maximum(acc[:, lane:lane + 1], 1e-30)


def _sb_tiles(jobs, uu):
    logits = [[_dot(qs, kt) for kt, _, _ in tiles] for qs, _, _, tiles in jobs]
    mids = []
    for (_, qpos_rows, _, tiles), zs in zip(jobs, logits):
        row = []
        for z, (_, _, kpos_cols) in zip(zs, tiles):
            causal = kpos_cols < qpos_rows
            softplus = jnp.maximum(z, 0.0) + jnp.log(1.0 + jnp.exp(-jnp.abs(z)))
            log_1mb = jnp.where(causal, -softplus, 0.0)
            hi, lo = _split_bf16(log_1mb, 2)
            row.append((causal, z - softplus, jnp.sum(log_1mb, axis=-1, keepdims=True),
                        jnp.concatenate([hi, lo], axis=1)))
        mids.append(row)
    between = [[_dot(m[3], uu) for m in row] for row in mids]
    weights, r_out = [], []
    for (_, _, (r, _), _), row, btw in zip(jobs, mids, between):
        ws = []
        for (causal, log_beta, total, _), b in zip(row, btw):
            ws.append(jnp.where(causal, jnp.exp(log_beta + b + r), 0.0).astype(BF16))
            r = r + total
        weights.append(ws)
        r_out.append(r)
    out = []
    for (_, _, (_, acc), tiles), ws, r in zip(jobs, weights, r_out):
        for w, (_, vt, _) in zip(ws, tiles):
            acc = acc + _dot_nt(w, vt)
        out.append((r, acc))
    return out


def _gate_rows(gate, g, j, nq):
    cols = [gate[:, (g * HPG + h) * 3 + j:(g * HPG + h) * 3 + j + 1] for h in range(HPG)]
    return jnp.concatenate(cols, axis=0)


def _lane_lt(shape, n):
    return lax.broadcasted_iota(jnp.int32, shape, 1) < n


def _sb_prompt_kernel(q_ref, kv_ref, uu_ref, o_ref, r_ref, acc_ref):
    qt = pl.program_id(1)
    nq = Q_TILE
    n_pair = H_SB // 2
    qpos = qt * nq + lax.broadcasted_iota(jnp.int32, (nq, 1), 0)
    qpos_rows = _rep_rows(qpos, 2)
    kcol = lax.broadcasted_iota(jnp.int32, (1, KEY_TILE), 1)
    r_ref[...] = jnp.zeros(r_ref.shape, F32)
    acc_ref[...] = jnp.zeros(acc_ref.shape, F32)

    def cond(c):
        return jnp.logical_and(c[0] <= qt, c[1] > SB_EXIT)

    def body(c):
        kt = qt - c[0]
        kpos = kt * KEY_TILE + kcol
        jobs = []
        for p in range(n_pair):
            qs = jnp.concatenate([q_ref[:, (2 * p) * LANES:(2 * p + 1) * LANES],
                                  q_ref[:, (2 * p + 1) * LANES:(2 * p + 2) * LANES]], axis=0)
            jobs.append((qs, qpos_rows, (r_ref[p], acc_ref[p]),
                         [(kv_ref[kt, p * LANES:(p + 1) * LANES, :],
                           kv_ref[kt, SB_W + p * LANES:SB_W + (p + 1) * LANES, :], kpos)]))
        r_max = None
        for p, (r, acc) in enumerate(_sb_tiles(jobs, uu_ref[...])):
            r_ref[p] = r
            acc_ref[p] = acc
            r_max = r if r_max is None else jnp.maximum(r_max, r)
        return c[0] + 1, jnp.max(r_max)

    lax.while_loop(cond, body, (jnp.int32(0), jnp.float32(0.0)))
    low = _lane_lt((nq, LANES), HEAD_DIM)
    for p in range(n_pair):
        acc = acc_ref[p]
        o_ref[:, p * LANES:(p + 1) * LANES] = jnp.where(low, acc[:nq], acc[nq:]).astype(BF16)


def _sb_prompt(q_sb_pad, sb_tiles, uu, batch, seq):
    nqt = seq // Q_TILE
    return pl.pallas_call(
        _sb_prompt_kernel,
        grid=(batch, nqt),
        in_specs=[pl.BlockSpec((Q_TILE, H_SB * LANES), lambda b, t: (b * nqt + t, 0)),
                  pl.BlockSpec((None,) + sb_tiles.shape[1:], lambda b, t: (b, 0, 0, 0)),
                  pl.BlockSpec(uu.shape, lambda b, t: (0, 0))],
        out_specs=pl.BlockSpec((Q_TILE, SB_W), lambda b, t: (b * nqt + t, 0)),
        out_shape=jax.ShapeDtypeStruct((batch * seq, SB_W), BF16),
        scratch_shapes=[pltpu.VMEM((H_SB // 2, 2 * Q_TILE, LANES), F32),
                        pltpu.VMEM((H_SB // 2, 2 * Q_TILE, LANES), F32)],
        compiler_params=_cparams(2),
        name="sb_prompt",
    )(q_sb_pad, sb_tiles, uu)


def _pos_rows(kpos):
    row = lax.broadcasted_iota(jnp.int32, (HEAD_DIM, kpos.shape[1]), 0)
    hi = jnp.right_shift(kpos, 8).astype(F32)
    lo = jnp.bitwise_and(kpos, 255).astype(F32)
    return jnp.where(row == 0, hi, jnp.where(row == 1, lo, 0.0)).astype(BF16)


def _nsa_prompt_kernel(q_ref, gate_ref, ht_ref, slc_ref, win_ref, mblk_ref, e_ref, o_ref,
                       qsel_ref, qwin_ref, part_ref, m_ref, acc_ref, *, n_sb):
    qt = pl.program_id(1)
    nq = Q_TILE
    qpos = qt * nq + lax.broadcasted_iota(jnp.int32, (nq, 1), 0)
    qpos_rows = _rep_rows(qpos, HPG)
    kcol = lax.broadcasted_iota(jnp.int32, (1, KEY_TILE), 1)
    gate = gate_ref[...]
    cmpk, cmpv = _cmp_tables(ht_ref[...])
    sel = []
    for g in range(G_NSA):
        heads = [g * HPG + h for h in range(HPG)]
        slots = [q_ref[:, h * LANES:(h + 1) * LANES] for h in heads]
        qs = _query_rows(slots, heads)
        qwin_ref[g] = qs
        p, o_c = _cmp_branch(qs, cmpk[g], cmpv, qpos_rows)
        part_ref[g] = _gate_rows(gate, g, 0, nq) * o_c
        imp = p[0:nq] + p[nq:2 * nq] + p[2 * nq:3 * nq] + p[3 * nq:4 * nq]
        sel.append(_select_blocks(imp, mblk_ref[...], qpos, n_sb, blocks_on_sublanes=True))
        qsel_ref[g] = _query_rows(slots, heads, extra=1.0 - sel[g])
    any_sel = jnp.max(jnp.maximum(sel[0], sel[1]), axis=0, keepdims=True)
    tile_of_block = jnp.right_shift(lax.broadcasted_iota(jnp.int32, any_sel.shape, 1),
                                    (KEY_TILE // SEL_BLOCK).bit_length() - 1)
    m_ref[...] = jnp.full(m_ref.shape, NEG, F32)
    acc_ref[...] = jnp.zeros(acc_ref.shape, F32)

    def tile_update(branches, kt, mask, n_tiles=1):
        wide = lambda parts: parts[0] if n_tiles == 1 else jnp.concatenate(parts, axis=1)
        pos = wide([_pos_rows((kt + t) * KEY_TILE + kcol) for t in range(n_tiles)])
        jobs, slots = [], []
        for branch in branches:
            tile_ref, q_scr = (slc_ref, qsel_ref) if branch == 0 else (win_ref, qwin_ref)
            keys = wide([tile_ref[kt + t, :KV_W, :] for t in range(n_tiles)])
            values = wide([tile_ref[kt + t, KV_W:, :] for t in range(n_tiles)])
            for g in range(G_NSA):
                s = branch * G_NSA + g
                k_tile = _key_tile(keys, g, pos, e_ref[kt] if branch == 0 else None)
                jobs.append((q_scr[g], k_tile, _value_tile(values, g), (m_ref[s], acc_ref[s])))
                slots.append(s)
        for s, (m, acc) in zip(slots, _softmax_tiles(jobs, mask)):
            m_ref[s] = m
            acc_ref[s] = acc

    def slc_body(kt, c):
        @pl.when(jnp.max(jnp.where(tile_of_block == kt, any_sel, 0.0)) > 0.5)
        def _():
            tile_update((0,), kt, None)

        return c

    lax.fori_loop(0, qt, slc_body, 0)

    n_back = WINDOW // KEY_TILE
    window_edge = lambda kt: (kt * KEY_TILE + kcol) > qpos_rows - WINDOW

    @pl.when(qt >= n_back)
    def _():
        tile_update((1,), qt - n_back, window_edge(qt - n_back), n_tiles=n_back)

    @pl.when(qt < n_back)
    def _():
        for back in range(n_back - 1, 0, -1):
            @pl.when(qt >= back)
            def _(back=back):
                tile_update((1,), qt - back, None)

    tile_update((0, 1), qt, (qt * KEY_TILE + kcol) <= qpos_rows)
    outs = [part_ref[g] + _gate_rows(gate, g, 1, nq) * _softmax_finish(acc_ref[g], g)
            + _gate_rows(gate, g, 2, nq) * _softmax_finish(acc_ref[G_NSA + g], g) for g in range(G_NSA)]
    low = _lane_lt((nq, LANES), HEAD_DIM)
    for h in range(HPG):
        o = jnp.where(low, outs[0][h * nq:(h + 1) * nq], outs[1][h * nq:(h + 1) * nq])
        o_ref[:, h * LANES:(h + 1) * LANES] = o.astype(BF16)


def _nsa_prompt(q_n_pad, gate, ht, slc_tiles, win_tiles, m_blk, e_tiles, batch, seq):
    nqt = seq // Q_TILE
    n_chunk = seq // CMP_STRIDE
    seq_blk = lambda a: pl.BlockSpec((None,) + a.shape[1:], lambda b, t: (b,) + (0,) * (a.ndim - 1))
    return pl.pallas_call(
        functools.partial(_nsa_prompt_kernel, n_sb=seq // SEL_BLOCK),
        grid=(batch, nqt),
        in_specs=[pl.BlockSpec((Q_TILE, H_NSA * LANES), lambda b, t: (b * nqt + t, 0)),
                  pl.BlockSpec((Q_TILE, LANES), lambda b, t: (b * nqt + t, 0)),
                  pl.BlockSpec((None, n_chunk, 4 * KV_W), lambda b, t: (b, 0, 0)),
                  seq_blk(slc_tiles), seq_blk(win_tiles),
                  pl.BlockSpec(m_blk.shape, lambda b, t: (0, 0)),
                  pl.BlockSpec(e_tiles.shape, lambda b, t: (0, 0, 0))],
        out_specs=pl.BlockSpec((Q_TILE, NSA_QW), lambda b, t: (b * nqt + t, 0)),
        out_shape=jax.ShapeDtypeStruct((batch * seq, NSA_QW), BF16),
        scratch_shapes=[pltpu.VMEM((G_NSA, HPG * Q_TILE, LANES + m_blk.shape[1]), BF16),
                        pltpu.VMEM((G_NSA, HPG * Q_TILE, LANES), BF16),
                        pltpu.VMEM((G_NSA, HPG * Q_TILE, LANES), F32),
                        pltpu.VMEM((2 * G_NSA, HPG * Q_TILE, LANES), F32),
                        pltpu.VMEM((2 * G_NSA, HPG * Q_TILE, LANES), F32)],
        compiler_params=_cparams(2),
        name="nsa_prompt",
    )(q_n_pad, gate, ht, slc_tiles, win_tiles, m_blk, e_tiles)


def _new_key_tiles(x):
    padded = jnp.concatenate([x, jnp.zeros((KEY_TILE - Q_PAD, x.shape[1]), F32)], axis=0)
    cols = [padded[:, c * LANES:(c + 1) * LANES].T for c in range(x.shape[1] // LANES)]
    return jnp.concatenate(cols, axis=0).astype(BF16)


def _nsa_sample_local_kernel(q_ref, gate_ref, ht_ref, win_ref, neww_ref, mblk_ref,
                             part_ref, sel_ref, *, past, n_sb):
    nq = Q_PAD
    rows = HPG * nq
    qpos = past + lax.broadcasted_iota(jnp.int32, (nq, 1), 0)
    qpos_rows = _rep_rows(qpos, HPG)
    kcol = lax.broadcasted_iota(jnp.int32, (1, KEY_TILE), 1)
    gate = gate_ref[...]
    cmpk, cmpv = _cmp_tables(ht_ref[...])
    n_w = win_ref.shape[1]
    qs, o_c = [], []
    for g in range(G_NSA):
        heads = [g * HPG + h for h in range(HPG)]
        qs.append(_query_rows([q_ref[h * nq:(h + 1) * nq, :] for h in heads], heads))
        p, oc = _cmp_branch(qs[g], cmpk[g], cmpv, qpos_rows)
        o_c.append(oc)
        imp = p[0:nq] + p[nq:2 * nq] + p[2 * nq:3 * nq] + p[3 * nq:4 * nq]
        sel_ref[g * nq:(g + 1) * nq, :] = _select_blocks(imp, mblk_ref[...], qpos, n_sb,
                                                         blocks_on_sublanes=False)
    tiles = [(win_ref[:, t * KEY_TILE:(t + 1) * KEY_TILE].astype(BF16), (past - n_w + t * KEY_TILE) + kcol)
             for t in range(n_w // KEY_TILE)]
    tiles.append((_new_key_tiles(neww_ref[...]), past + kcol))
    carry = [_softmax_init(rows) for _ in range(G_NSA)]
    for tile, kpos in tiles:
        dist = qpos_rows - kpos
        mask = jnp.logical_and(jnp.logical_and(dist >= 0, dist < WINDOW), kpos >= 0)
        pos = _pos_rows(kpos)
        carry = _softmax_tiles([(qs[g], _key_tile(tile[:KV_W], g, pos), _value_tile(tile[KV_W:], g), carry[g])
                                for g in range(G_NSA)], mask)
    for g in range(G_NSA):
        part_ref[g * rows:(g + 1) * rows, :] = (_gate_rows(gate, g, 0, nq) * o_c[g]
                                                + _gate_rows(gate, g, 2, nq) * _softmax_finish(carry[g][1], g))


def _nsa_sample_local(q_n_rows, gate8, ht, win_state_t, new_win8, m_blk, past, n_sb):
    n_seq = q_n_rows.shape[0]
    blk = lambda a: pl.BlockSpec((None,) + a.shape[1:], lambda b: (b,) + (0,) * (a.ndim - 1))
    nbp = m_blk.shape[1]
    return pl.pallas_call(
        functools.partial(_nsa_sample_local_kernel, past=past, n_sb=n_sb),
        grid=(n_seq,),
        in_specs=[blk(q_n_rows), blk(gate8), blk(ht), blk(win_state_t), blk(new_win8),
                  pl.BlockSpec(m_blk.shape, lambda b: (0, 0))],
        out_specs=[pl.BlockSpec((None, H_NSA * Q_PAD, LANES), lambda b: (b, 0, 0)),
                   pl.BlockSpec((None, G_NSA * Q_PAD, nbp), lambda b: (b, 0, 0))],
        out_shape=[jax.ShapeDtypeStruct((n_seq, H_NSA * Q_PAD, LANES), F32),
                   jax.ShapeDtypeStruct((n_seq, G_NSA * Q_PAD, nbp), F32)],
        compiler_params=_cparams(1),
        name="nsa_sample_local",
    )(q_n_rows, gate8, ht, win_state_t, new_win8, m_blk)


def _sb_sample_kernel(*refs, per_step, top_page, past, first_phase):
    live_ref = refs[1]
    pages = refs[2:2 + per_step]
    qsb_ref, newsb_ref, rin_ref, accin_ref, uu_ref, r_ref, acc_ref, on_ref = refs[2 + per_step:]
    b = pl.program_id(0)
    j = pl.program_id(1)
    nq = Q_PAD
    n_pair = H_SB // 2
    qpos2 = _rep_rows(past + lax.broadcasted_iota(jnp.int32, (nq, 1), 0), 2)
    kcol = lax.broadcasted_iota(jnp.int32, (1, KEY_TILE), 1)

    def visit(tiles):
        jobs = [(qsb_ref[pair * 2 * nq:(pair + 1) * 2 * nq, :], qpos2, (r_ref[pair], acc_ref[pair]),
                 [(tile[pair * LANES:(pair + 1) * LANES], tile[SB_W + pair * LANES:SB_W + (pair + 1) * LANES],
                   kpos) for tile, kpos in tiles]) for pair in range(n_pair)]
        r_max = None
        for pair, (r, acc) in enumerate(_sb_tiles(jobs, uu_ref[...])):
            r_ref[pair] = r
            acc_ref[pair] = acc
            r_max = r if r_max is None else jnp.maximum(r_max, r)
        on_ref[0] = (jnp.max(r_max) > SB_EXIT).astype(jnp.int32)

    @pl.when(j == 0)
    def _():
        r_ref[...] = rin_ref[...]
        acc_ref[...] = accin_ref[...]
        if first_phase:
            visit([(_new_key_tiles(newsb_ref[...]), past + kcol)])
        else:
            on_ref[0] = live_ref[b]

    group = 2
    for k0 in range(0, per_step, group):
        @pl.when(on_ref[0] != 0)
        def _(k0=k0):
            tiles = []
            for k in range(k0, min(k0 + group, per_step)):
                pg = top_page - (j * per_step + k)
                tiles.append((pages[k][...].astype(BF16), pg * KEY_TILE + kcol))
            visit(tiles)


def _sb_sample_phase(pages_flat, live, cache_sb_t, q_sb_rows, new_sb8, r_in, acc_in, uu, *, n_pages,
                     top_page, n_visit, per_step, past, first_phase):
    n_seq = q_sb_rows.shape[0]
    n_steps = n_visit // per_step

    def page_spec(k):
        def index(b, j, pt, lv):
            own = pt[b * n_pages + (top_page - (j * per_step + k))]
            return (jnp.where(lv[b] != 0, own, pt[0]), 0, 0)
        return pl.BlockSpec((None, 2 * SB_W, KEY_TILE), index)

    seq_blk = lambda a: pl.BlockSpec((None,) + a.shape[1:], lambda b, j, pt, lv: (b,) + (0,) * (a.ndim - 1))
    grid_spec = pltpu.PrefetchScalarGridSpec(
        num_scalar_prefetch=2,
        grid=(n_seq, n_steps),
        in_specs=[page_spec(k) for k in range(per_step)]
        + [seq_blk(q_sb_rows), seq_blk(new_sb8), seq_blk(r_in), seq_blk(acc_in),
           pl.BlockSpec(uu.shape, lambda b, j, pt, lv: (0, 0))],
        out_specs=[seq_blk(r_in), seq_blk(acc_in)],
        scratch_shapes=[pltpu.SMEM((1,), jnp.int32)],
    )
    return pl.pallas_call(
        functools.partial(_sb_sample_kernel, per_step=per_step, top_page=top_page, past=past,
                          first_phase=first_phase),
        grid_spec=grid_spec,
        out_shape=[jax.ShapeDtypeStruct(r_in.shape, F32), jax.ShapeDtypeStruct(acc_in.shape, F32)],
        compiler_params=_cparams(2),
        name="sb_sample_first" if first_phase else "sb_sample_rest",
    )(pages_flat, live, *([cache_sb_t] * per_step), q_sb_rows, new_sb8, r_in, acc_in, uu)


def _largest_divisor(n, cap):
    return max(d for d in range(1, cap + 1) if n % d == 0)


def _sb_sample(pages_flat, cache_sb_t, q_sb_rows, new_sb8, uu, n_pages, past):
    n_seq = q_sb_rows.shape[0]
    n_pair = H_SB // 2
    n_first = _largest_divisor(n_pages, 4)
    zeros_r = jnp.zeros((n_seq, n_pair, 2 * Q_PAD, LANES), F32)
    zeros_acc = jnp.zeros((n_seq, n_pair, 2 * Q_PAD, LANES), F32)
    everyone = jnp.ones((n_seq,), jnp.int32)
    common = dict(n_pages=n_pages, past=past)
    r, acc = _sb_sample_phase(pages_flat, everyone, cache_sb_t, q_sb_rows, new_sb8, zeros_r, zeros_acc, uu,
                              top_page=n_pages - 1, n_visit=n_first, per_step=n_first, first_phase=True, **common)
    n_rest = n_pages - n_first
    if n_rest:
        live = (jnp.max(r, axis=(1, 2, 3)) > SB_EXIT).astype(jnp.int32)
        r, acc = _sb_sample_phase(pages_flat, live, cache_sb_t, q_sb_rows, new_sb8, r, acc, uu,
                                  top_page=n_pages - 1 - n_first, n_visit=n_rest,
                                  per_step=_largest_divisor(n_rest, 30), first_phase=False, **common)
    low = jnp.arange(LANES) < HEAD_DIM
    o = jnp.where(low, acc[:, :, :Q_PAD], acc[:, :, Q_PAD:])
    return o.transpose(0, 2, 1, 3).reshape(n_seq, Q_PAD, SB_W).astype(BF16)


def _slc_sample_kernel(*refs, per_step, n_pages, past):
    active_ref = refs[1]
    nsa_pages = refs[2:2 + per_step]
    (qn_ref, newslc_ref, sel_ref, part_ref, gate_ref, e_ref, onsa_ref, m_ref, acc_ref) = refs[2 + per_step:]
    b = pl.program_id(0)
    j = pl.program_id(1)
    nq = Q_PAD
    rows = HPG * nq
    qpos4 = _rep_rows(past + lax.broadcasted_iota(jnp.int32, (nq, 1), 0), HPG)
    kcol = lax.broadcasted_iota(jnp.int32, (1, KEY_TILE), 1)

    def visit(tiles, mask):
        wide = lambda parts: parts[0] if len(parts) == 1 else jnp.concatenate(parts, axis=1)
        kt = wide([t[0][:KV_W] for t in tiles])
        vt = wide([t[0][KV_W:] for t in tiles])
        pos = wide([_pos_rows(t[2]) for t in tiles])
        sel_rows = wide([t[1] for t in tiles])
        jobs = []
        for g in range(G_NSA):
            heads = [g * HPG + h for h in range(HPG)]
            not_sel = 1.0 - sel_ref[g * nq:(g + 1) * nq, :]
            q = _query_rows([qn_ref[h * nq:(h + 1) * nq, :] for h in heads], heads, extra=not_sel)
            jobs.append((q, _key_tile(kt, g, pos, sel_rows), _value_tile(vt, g), (m_ref[g], acc_ref[g])))
        for g, (m, acc) in enumerate(_softmax_tiles(jobs, mask)):
            m_ref[g] = m
            acc_ref[g] = acc

    @pl.when(j == 0)
    def _():
        m_ref[...] = jnp.full(m_ref.shape, NEG, F32)
        acc_ref[...] = jnp.zeros(acc_ref.shape, F32)
        visit([(_new_key_tiles(newslc_ref[...]), e_ref[n_pages], past + kcol)], (past + kcol) <= qpos4)

    @pl.when(active_ref[b * pl.num_programs(1) + j] != 0)
    def _():
        tiles = []
        for k in range(per_step):
            pg = n_pages - 1 - (j * per_step + k)
            tiles.append((nsa_pages[k][...].astype(BF16), e_ref[pg], pg * KEY_TILE + kcol))
        visit(tiles, None)

    @pl.when(j == pl.num_programs(1) - 1)
    def _():
        low = _lane_lt((nq, LANES), HEAD_DIM)
        gate = gate_ref[...]
        outs = [part_ref[g * rows:(g + 1) * rows, :] + _gate_rows(gate, g, 1, nq) * _softmax_finish(acc_ref[g], g)
                for g in range(G_NSA)]
        for h in range(HPG):
            o = jnp.where(low, outs[0][h * nq:(h + 1) * nq], outs[1][h * nq:(h + 1) * nq])
            onsa_ref[:, h * LANES:(h + 1) * LANES] = o.astype(BF16)


def _slc_sample(pages_flat, need, cache_nsa_t, q_n_rows, new_slc8, sel, part, gate8, e_tiles, n_pages, past):
    n_seq = q_n_rows.shape[0]
    per_step = _largest_divisor(n_pages, 16)
    n_steps = n_pages // per_step
    by_step = need[:, ::-1].reshape(n_seq, n_steps, per_step)
    step = jnp.arange(n_steps, dtype=jnp.int32)[None, :, None]
    shown_step = jnp.maximum(lax.cummax(jnp.where(by_step != 0, step, -1), axis=1), 0)
    slot = jnp.arange(per_step, dtype=jnp.int32)[None, None, :]
    shown_page = n_pages - 1 - (shown_step * per_step + slot)
    fetch = jnp.take_along_axis(pages_flat.reshape(n_seq, n_pages), shown_page.reshape(n_seq, -1), axis=1)
    fetch_flat = fetch.reshape(-1).astype(jnp.int32)
    active_flat = jnp.max(by_step, axis=2).reshape(-1).astype(jnp.int32)

    def nsa_spec(k):
        return pl.BlockSpec((None, 2 * KV_W, KEY_TILE),
                            lambda b, j, ft, nd: (ft[(b * n_steps + j) * per_step + k], 1, 0))

    seq_blk = lambda a: pl.BlockSpec((None,) + a.shape[1:], lambda b, j, ft, nd: (b,) + (0,) * (a.ndim - 1))
    grid_spec = pltpu.PrefetchScalarGridSpec(
        num_scalar_prefetch=2,
        grid=(n_seq, n_steps),
        in_specs=[nsa_spec(k) for k in range(per_step)]
        + [seq_blk(q_n_rows), seq_blk(new_slc8), seq_blk(sel), seq_blk(part), seq_blk(gate8),
           pl.BlockSpec(e_tiles.shape, lambda b, j, ft, nd: (0, 0, 0))],
        out_specs=pl.BlockSpec((None, Q_PAD, NSA_QW), lambda b, j, ft, nd: (b, 0, 0)),
        scratch_shapes=[pltpu.VMEM((G_NSA, HPG * Q_PAD, LANES), F32),
                        pltpu.VMEM((G_NSA, HPG * Q_PAD, LANES), F32)],
    )
    return pl.pallas_call(
        functools.partial(_slc_sample_kernel, per_step=per_step, n_pages=n_pages, past=past),
        grid_spec=grid_spec,
        out_shape=jax.ShapeDtypeStruct((n_seq, Q_PAD, NSA_QW), BF16),
        compiler_params=_cparams(2),
        name="slc_sample",
    )(fetch_flat, active_flat, *([cache_nsa_t] * per_step), q_n_rows, new_slc8, sel, part, gate8, e_tiles)


def _block_sum_matrix(n_chunk_pad, n_c, n_sb, nbp):
    per = SEL_BLOCK // CMP_STRIDE
    c = jnp.arange(n_chunk_pad)[:, None]
    b = jnp.arange(nbp)[None, :]
    own = (c // per == b)
    nxt = jnp.logical_and((c + 1) // per == b, c + 1 < n_sb * per)
    m = (own.astype(F32) + nxt.astype(F32)) * (c < n_c) * (b < n_sb)
    return m.astype(BF16)


def _expand_tiles(n_tiles, nbp):
    t = jnp.arange(n_tiles)[:, None, None]
    b = jnp.arange(nbp)[None, :, None]
    k = jnp.arange(KEY_TILE)[None, None, :]
    return (((t * KEY_TILE + k) // SEL_BLOCK == b) * MASKED).astype(BF16)


def _suffix_matrix():
    j = jnp.arange(KEY_TILE)[:, None]
    s = jnp.arange(KEY_TILE)[None, :]
    u = (j > s).astype(BF16)
    return jnp.concatenate([u, u], axis=0)


def _rows_per_seq(x, n_seq, n_new, slots):
    x = x.reshape(n_seq, n_new, slots, LANES)
    x = jnp.pad(x, ((0, 0), (0, Q_PAD - n_new), (0, 0), (0, 0)))
    return x.transpose(0, 2, 1, 3).reshape(n_seq, slots * Q_PAD, LANES)


def _pad_new(x, n_seq, n_new):
    x = x.reshape(n_seq, n_new, x.shape[-1])
    return jnp.pad(x, ((0, 0), (0, Q_PAD - n_new), (0, 0)))


def _token_minor(x, lead):
    n = x.ndim
    perm = tuple(range(lead)) + tuple(range(lead + 1, n)) + (lead,)
    x = x.transpose(perm)
    return x.reshape(x.shape[:lead] + (-1, x.shape[-1]))


def _token_major(x_t, lead, feat_shape):
    x = x_t.reshape(x_t.shape[:lead] + tuple(feat_shape) + (x_t.shape[-1],))
    n = x.ndim
    perm = tuple(range(lead)) + (n - 1,) + tuple(range(lead, n - 1))
    return x.transpose(perm)


def kernel(x_prompt, x_sample, cache_sb_kv, cache_nsa_kv, state_win_kv, page_table, c_prompt, c_sample,
           w_ada, b_ada, w_ff1_up, w_ff1_down, w_in, w_cmp_k, w_cmp_v, w_o, w_ff2_up, w_ff2_down,
           ln_g, ln_b):
    batch, seq, d = x_prompt.shape
    n_seq, n_new, _ = x_sample.shape
    depth, n_pool, page = cache_sb_kv.shape[:3]
    n_pages = page_table.shape[1]
    past = n_pages * page
    n_w = state_win_kv.shape[2]
    alpha = (2.0 * depth) ** 0.25
    assert page == KEY_TILE and seq % TOKEN_TILE == 0 and n_new <= Q_PAD and n_w % KEY_TILE == 0
    assert n_new < CMP_STRIDE and past % SEL_BLOCK == 0

    uu = _suffix_matrix()
    n_chunk_p = seq // CMP_STRIDE
    n_sb_p = seq // SEL_BLOCK
    m_blk_p = _block_sum_matrix(n_chunk_p, n_chunk_p - 1, n_sb_p, LANES * pl.cdiv(n_sb_p, LANES))
    e_p = _expand_tiles(seq // KEY_TILE, m_blk_p.shape[1])
    t_all = past + n_new
    n_sb_s = pl.cdiv(t_all, SEL_BLOCK)
    n_chunk_s = past // CMP_STRIDE
    m_blk_s = _block_sum_matrix(n_chunk_s, t_all // CMP_STRIDE - 1, n_sb_s,
                                LANES * pl.cdiv(n_sb_s, LANES))
    e_s = _expand_tiles(n_pages + 1, m_blk_s.shape[1])
    blocks_per_page = KEY_TILE // SEL_BLOCK

    xp = x_prompt.reshape(batch * seq, d)
    xs = x_sample.reshape(n_seq * n_new, d)
    c_all = jnp.concatenate([c_prompt, c_sample], axis=0)
    cache_sb_t = _token_minor(cache_sb_kv, 2).reshape(depth * n_pool, 2 * SB_W, page)
    cache_nsa_t = _token_minor(cache_nsa_kv, 2).reshape(depth * n_pool, 4 * KV_W, page)
    win_state_t = _token_minor(state_win_kv, 2)
    tiles_per_seq = seq // TOKEN_TILE
    states = [[] for _ in range(6)]

    for layer in range(depth):
        mod = _ada(c_all, w_ada[layer], b_ada[layer]).reshape(batch + n_seq, N_MOD, d)
        mod_p = mod[:batch].transpose(1, 0, 2)[:, :, None, :]
        mod_s = jnp.repeat(mod[batch:], n_new, axis=0).transpose(1, 0, 2)[:, None, :, :]
        w_up1, w_dn1 = w_ff1_up[layer].astype(BF16), w_ff1_down[layer].astype(BF16)
        w_up2, w_dn2 = w_ff2_up[layer].astype(BF16), w_ff2_down[layer].astype(BF16)
        w_proj = _proj_weights(w_in[layer])
        w_kv_t = w_proj[:, _P_KVSB:].T
        w_cmp = _compress_weights(w_cmp_k[layer], w_cmp_v[layer])
        w_osb = w_o[layer, :SB_W].astype(BF16)
        w_onsa = (w_o[layer, SB_W:].reshape(G_NSA, HPG, HEAD_DIM, d).transpose(1, 0, 2, 3)
                  .reshape(NSA_QW, d).astype(BF16))
        lng, lnb = ln_g[layer], ln_b[layer]
        pages_flat = (page_table + layer * n_pool).reshape(-1).astype(jnp.int32)

        kw_p = dict(per_token=False, tiles_per_seq=tiles_per_seq)
        x1 = _ffn(xp, mod_p, w_up1, w_dn1, lng[0:1], lnb[0:1], mod_base=0, alpha=alpha, **kw_p)
        (q_sb, q_n, gate, cmp_rows, kv_sb_t, kv_nsa_t, kv_win_t, sb_tiles, slc_tiles, win_tiles) = _proj_prompt(
            x1, mod_p, w_proj, w_kv_t, batch, seq, mod_base=3)
        ht = _compress_prompt(cmp_rows, w_cmp, batch, seq)
        o_sb = _sb_prompt(q_sb, sb_tiles, uu, batch, seq)
        o_nsa = _nsa_prompt(q_n, gate, ht, slc_tiles, win_tiles, m_blk_p, e_p, batch, seq)
        xp = _ffn(x1, mod_p, w_up2, w_dn2, lng[1:3], lnb[1:3], mod_base=6, alpha=alpha,
                  oproj=(o_sb, o_nsa, w_osb, w_onsa), **kw_p)
        n_keep = min(WINDOW, seq)
        states[0].append(_token_major(kv_sb_t, 1, (2, H_SB, HEAD_DIM)))
        states[1].append(_token_major(kv_nsa_t, 1, (4, G_NSA, HEAD_DIM)))
        states[2].append(_token_major(kv_win_t[:, :, seq - n_keep:], 1, (2, G_NSA, HEAD_DIM)))

        kw_s = dict(per_token=True, tiles_per_seq=1)
        s1 = _ffn(xs, mod_s, w_up1, w_dn1, lng[0:1], lnb[0:1], mod_base=0, alpha=alpha, **kw_s)
        q_sb, q_n, gate, kv_sb, kv_nsa, kv_win = _proj_sample(s1, mod_s, w_proj, mod_base=3)
        ht = _compress_pages(cache_nsa_t, pages_flat, w_cmp, n_seq, n_pages)
        q_n_rows = _rows_per_seq(q_n, n_seq, n_new, H_NSA)
        q_sb_rows = _rows_per_seq(q_sb, n_seq, n_new, H_SB)
        gate8 = _pad_new(gate, n_seq, n_new)
        part, sel = _nsa_sample_local(q_n_rows, gate8, ht, win_state_t[layer],
                                      _pad_new(kv_win, n_seq, n_new), m_blk_s, past, n_sb_s)
        page_sel = sel[:, :, :n_pages * blocks_per_page].reshape(n_seq, -1, n_pages, blocks_per_page)
        need = (jnp.max(page_sel, axis=(1, 3)) > 0.5).astype(jnp.int32)
        o_sb = _sb_sample(pages_flat, cache_sb_t, q_sb_rows, _pad_new(kv_sb, n_seq, n_new), uu, n_pages, past)
        o_nsa = _slc_sample(pages_flat, need, cache_nsa_t, q_n_rows, _pad_new(kv_nsa[:, 2 * KV_W:], n_seq, n_new),
                            sel, part, gate8, e_s, n_pages, past)
        o_sb = o_sb[:, :n_new].reshape(n_seq * n_new, SB_W)
        o_nsa = o_nsa[:, :n_new].reshape(n_seq * n_new, NSA_QW)
        xs = _ffn(s1, mod_s, w_up2, w_dn2, lng[1:3], lnb[1:3], mod_base=6, alpha=alpha,
                  oproj=(o_sb, o_nsa, w_osb, w_onsa), **kw_s)
        new_win_t = jnp.concatenate([win_state_t[layer][:, :, n_new:],
                                     kv_win.reshape(n_seq, n_new, 2 * KV_W).transpose(0, 2, 1)], axis=2)
        states[3].append(kv_sb.reshape(n_seq, n_new, 2, H_SB, HEAD_DIM))
        states[4].append(kv_nsa.reshape(n_seq, n_new, 4, G_NSA, HEAD_DIM))
        states[5].append(_token_major(new_win_t, 1, (2, G_NSA, HEAD_DIM)))

    return (xp.reshape(batch, seq, d), xs.reshape(n_seq, n_new, d)) + tuple(jnp.stack(s) for s in states)
```
